```python
import math
import jax, jax.numpy as jnp
from jax import lax
import numpy as np

D_MODEL = 2048
BATCH = 4
SEQ = 8192
DEPTH = 1

CTX_LEN = 256
GRID_W = 64

F_GROUPS = 4
F_GROUP_DIM = 256
F_WIDTH = F_GROUPS * F_GROUP_DIM
S_GROUP_DIM = 16
S_GROUPS = 32
S_WIDTH = S_GROUPS * S_GROUP_DIM
S_STATE = 64
DT_MIN = 1e-3
DT_MAX = 1e-1
N_DIRECTIONS = 2
N_BRANCH = 2
IN_WIDTH = F_WIDTH + S_WIDTH + N_BRANCH * D_MODEL
N_EXPERT_GROUPS = 4
EXPERTS_PER_GROUP = 8
N_EXPERTS = N_EXPERT_GROUPS * EXPERTS_PER_GROUP
TOP_K = 2
D_EXPERT = 1024
ROW_BLOCK = 128
N_MOD = 6
LN_EPS = 1e-5
DEEPNORM_ALPHA = (2.0 * DEPTH) ** 0.25
DEEPNORM_BETA = (8.0 * DEPTH) ** -0.25

kernel_name = "hybrid_fourier_s5_hmoe_dit"

F32 = jnp.float32


def layer_norm(x, gain=None, bias=None):
    xf = x.astype(F32)
    mu = xf.mean(-1, keepdims=True)
    var = jnp.square(xf - mu).mean(-1, keepdims=True)
    y = (xf - mu) * lax.rsqrt(var + LN_EPS)
    if gain is not None:
        y = y * gain.astype(F32) + bias.astype(F32)
    return y.astype(x.dtype)


def modulate(x, shift, scale):
    return layer_norm(x) * (1 + scale[:, None]) + shift[:, None]


def fourier_mix(u, pos_shape):
    b = u.shape[0]
    uf = u.astype(F32).reshape((b,) + pos_shape + (F_GROUPS, F_GROUP_DIM))
    axes = tuple(range(1, 1 + len(pos_shape))) + (uf.ndim - 1,)
    y = jnp.fft.fftn(uf, axes=axes, norm="ortho").real
    return y.reshape(u.shape).astype(u.dtype)


def _linear_recurrence(left, right):
    a_l, b_l = left
    a_r, b_r = right
    return a_r * a_l, a_r * b_l + b_r


def s5_states(u, lam_re, lam_im, log_dt, b_re, b_im, s0):
    lam = lax.complex(lam_re.astype(F32), lam_im.astype(F32))
    dt = jnp.exp(log_dt.astype(F32))[:, None]
    lam_bar = jnp.exp(lam * dt)
    b_cplx = lax.complex(b_re.astype(F32), b_im.astype(F32))
    b_bar = ((lam_bar - 1.0) / lam)[:, :, None] * b_cplx
    bu = jnp.einsum("gph,blgh->blgp", b_bar, u.astype(jnp.complex64))
    if s0 is not None:
        bu = bu.at[:, 0].add(lam_bar * s0)
    a = jnp.broadcast_to(lam_bar, bu.shape)
    _, states = lax.associative_scan(_linear_recurrence, (a, bu), axis=1)
    return states


def s5_readout(states, c_re, c_im):
    c_cplx = lax.complex(c_re.astype(F32), c_im.astype(F32))
    return jnp.einsum("ghp,blgp->blgh", c_cplx, states).real


def s5_glu(y, w_glu, b_glu):
    y = jax.nn.gelu(y.reshape(y.shape[0], y.shape[1], S_WIDTH))
    return y * jax.nn.sigmoid(y @ w_glu.astype(F32) + b_glu.astype(F32))


def merge_branches(f_out, s_out, z_f, z_s, w_f_proj, w_s_proj, w_out):
    m = jax.nn.sigmoid(z_f) * (f_out @ w_f_proj) + jax.nn.sigmoid(z_s) * (s_out @ w_s_proj)
    return m @ w_out


def hierarchical_route(h, w_group, b_group, w_expert, b_expert):
    t = h.shape[0]
    g_prob = jax.nn.softmax((h @ w_group + b_group).astype(F32), axis=-1)
    g_top, g_idx = lax.top_k(g_prob, 1)
    e_logits = (h @ w_expert + b_expert).astype(F32).reshape(t, N_EXPERT_GROUPS, EXPERTS_PER_GROUP)
    e_logits = jnp.take_along_axis(e_logits, g_idx[:, :, None], axis=1)[:, 0]
    e_top, e_local = lax.top_k(jax.nn.softmax(e_logits, axis=-1), TOP_K)
    weights = g_top * e_top / e_top.sum(-1, keepdims=True)
    experts = g_idx * EXPERTS_PER_GROUP + e_local
    return experts, weights


def expert_dispatch(h, experts, weights, w1, w3, w2):
    t, d = h.shape
    n_assign = t * TOP_K
    e_flat = experts.reshape(-1)
    w_flat = weights.reshape(-1)
    tok_flat = jnp.repeat(jnp.arange(t, dtype=jnp.int32), TOP_K)
    counts = jnp.bincount(e_flat, length=N_EXPERTS)
    padded = (counts + ROW_BLOCK - 1) // ROW_BLOCK * ROW_BLOCK
    start = jnp.cumsum(counts) - counts
    pend = jnp.cumsum(padded)
    pstart = pend - padded
    order = jnp.argsort(e_flat)
    e_sorted = e_flat[order]
    dest = pstart[e_sorted] + jnp.arange(n_assign) - start[e_sorted]
    n_blocks = -(-(n_assign + N_EXPERTS * (ROW_BLOCK - 1)) // ROW_BLOCK)
    n_rows = n_blocks * ROW_BLOCK
    row_tok = jnp.full((n_rows,), t, jnp.int32).at[dest].set(tok_flat[order])
    row_w = jnp.zeros((n_rows,), F32).at[dest].set(w_flat[order])
    block_exp = jnp.minimum(
        jnp.searchsorted(pend, jnp.arange(n_blocks) * ROW_BLOCK, side="right"), N_EXPERTS - 1)
    h_pad = jnp.concatenate([h, jnp.zeros((1, d), h.dtype)], axis=0)
    xb = h_pad[row_tok].reshape(n_blocks, ROW_BLOCK, d)

    def run_block(args):
        xblk, e = args
        return (jax.nn.silu(xblk @ w1[e]) * (xblk @ w3[e])) @ w2[e]

    yb = lax.map(run_block, (xb, block_exp)).reshape(n_rows, d)
    out = jnp.zeros_like(h_pad).at[row_tok].add(yb * row_w[:, None].astype(yb.dtype))
    return out[:t]


def hierarchical_moe(h, w_group, b_group, w_expert, b_expert, w1, w3, w2):
    experts, weights = hierarchical_route(h, w_group, b_group, w_expert, b_expert)
    return expert_dispatch(h, experts, weights, w1, w3, w2)


def setup_inputs(seed: int = 0) -> dict:
    key = jax.random.key(seed)
    ks = jax.random.split(key, 32)
    nrm = jax.random.normal
    L, D = DEPTH, D_MODEL
    G, P, H = S_GROUPS, S_STATE, S_GROUP_DIM
    lam_im0 = jnp.pi * jnp.arange(P, dtype=F32)
    return {
        "x": nrm(ks[0], (BATCH, SEQ, D), F32),
        "c": nrm(ks[1], (BATCH, D), F32),
        "ctx": nrm(ks[2], (BATCH, CTX_LEN, D), F32),
        "c_ctx": nrm(ks[3], (D,), F32),
        "w_ada": nrm(ks[4], (L, D, N_MOD * D), F32) * D ** -0.5,
        "b_ada": 0.01 * nrm(ks[5], (L, N_MOD * D), F32),
        "w_in": nrm(ks[6], (L, D, IN_WIDTH), F32) * D ** -0.5,
        "w_f_proj": nrm(ks[7], (L, F_WIDTH, D), F32) * F_WIDTH ** -0.5,
        "lam_re": -0.5 * (1.0 + 0.01 * nrm(ks[8], (L, N_DIRECTIONS, G, P), F32)),
        "lam_im": lam_im0 + 0.01 * nrm(ks[9], (L, N_DIRECTIONS, G, P), F32),
        "log_dt": jax.random.uniform(ks[10], (L, N_DIRECTIONS, G), F32,
                                     math.log(DT_MIN), math.log(DT_MAX)),
        "b_re": nrm(ks[11], (L, N_DIRECTIONS, G, P, H), F32) * (2.0 * H) ** -0.5,
        "b_im": nrm(ks[12], (L, N_DIRECTIONS, G, P, H), F32) * (2.0 * H) ** -0.5,
        "c_re": nrm(ks[13], (L, N_DIRECTIONS, G, H, P), F32) * (2.0 * P) ** -0.5,
        "c_im": nrm(ks[14], (L, N_DIRECTIONS, G, H, P), F32) * (2.0 * P) ** -0.5,
        "d_skip": nrm(ks[15], (L, G, H), F32),
        "w_glu": nrm(ks[16], (L, S_WIDTH, S_WIDTH), F32) * S_WIDTH ** -0.5,
        "b_glu": 0.01 * nrm(ks[17], (L, S_WIDTH), F32),
        "w_s_proj": nrm(ks[18], (L, S_WIDTH, D), F32) * S_WIDTH ** -0.5,
        "w_out": nrm(ks[19], (L, D, D), F32) * D ** -0.5 * DEEPNORM_BETA,
        "ln1_g": 1.0 + 0.01 * nrm(ks[20], (L, D), F32),
        "ln1_b": 0.01 * nrm(ks[21], (L, D), F32),
        "w_group": nrm(ks[22], (L, D, N_EXPERT_GROUPS), F32) * D ** -0.5,
        "b_group": 0.01 * nrm(ks[23], (L, N_EXPERT_GROUPS), F32),
        "w_expert": nrm(ks[24], (L, D, N_EXPERTS), F32) * D ** -0.5,
        "b_expert": 0.01 * nrm(ks[25], (L, N_EXPERTS), F32),
        "w1": nrm(ks[26], (L, N_EXPERTS, D, D_EXPERT), F32) * D ** -0.5,
        "w3": nrm(ks[27], (L, N_EXPERTS, D, D_EXPERT), F32) * D ** -0.5,
        "w2": nrm(ks[28], (L, N_EXPERTS, D_EXPERT, D), F32) * D_EXPERT ** -0.5 * DEEPNORM_BETA,
        "ln2_g": 1.0 + 0.01 * nrm(ks[29], (L, D), F32),
        "ln2_b": 0.01 * nrm(ks[30], (L, D), F32),
    }


def reference(x, c, ctx, c_ctx, w_ada, b_ada, w_in, w_f_proj, lam_re, lam_im, log_dt,
              b_re, b_im, c_re, c_im, d_skip, w_glu, b_glu, w_s_proj, w_out,
              ln1_g, ln1_b, w_group, b_group, w_expert, b_expert, w1, w3, w2,
              ln2_g, ln2_b):
    bsz, seq, _ = x.shape
    ctx_len = ctx.shape[1]
    rows = seq // GRID_W
    ctx_x = ctx
    cut = (F_WIDTH, F_WIDTH + S_WIDTH, F_WIDTH + S_WIDTH + D_MODEL)
    for l in range(DEPTH):
        last = l == DEPTH - 1
        mod = jnp.split(jax.nn.silu(c) @ w_ada[l] + b_ada[l], N_MOD, axis=-1)
        mod_c = jnp.split(jax.nn.silu(c_ctx)[None] @ w_ada[l] + b_ada[l], N_MOD, axis=-1)
        ssm_dir = [(lam_re[l, d], lam_im[l, d], log_dt[l, d], b_re[l, d], b_im[l, d])
                   for d in range(N_DIRECTIONS)]

        h = modulate(x, mod[0], mod[1])
        u_f, u_s, z_f, z_s = jnp.split(h @ w_in[l], cut, axis=-1)
        hc = modulate(ctx_x, mod_c[0], mod_c[1])
        if last:
            uc_s = hc @ w_in[l, :, F_WIDTH:F_WIDTH + S_WIDTH]
        else:
            uc_f, uc_s, zc_f, zc_s = jnp.split(hc @ w_in[l], cut, axis=-1)
        uc_s = uc_s.astype(F32).reshape(bsz, ctx_len, S_GROUPS, S_GROUP_DIM)
        sc_f = s5_states(uc_s, *ssm_dir[0], None)
        sc_b = s5_states(uc_s[:, ::-1], *ssm_dir[1], None)
        u_s = u_s.astype(F32).reshape(bsz, seq, S_GROUPS, S_GROUP_DIM)
        s_f = s5_states(u_s, *ssm_dir[0], sc_f[:, -1])
        s_b = s5_states(u_s[:, ::-1], *ssm_dir[1], sc_b[:, -1])[:, ::-1]
        y_s = (s5_readout(s_f, c_re[l, 0], c_im[l, 0]) + s5_readout(s_b, c_re[l, 1], c_im[l, 1])
               + d_skip[l] * u_s)
        s_out = s5_glu(y_s, w_glu[l], b_glu[l]).astype(x.dtype)
        f_out = fourier_mix(u_f, (rows, GRID_W))
        mix = merge_branches(f_out, s_out, z_f, z_s, w_f_proj[l], w_s_proj[l], w_out[l])
        x_mid = layer_norm(DEEPNORM_ALPHA * x + mod[2][:, None] * mix, ln1_g[l], ln1_b[l])
        if not last:
            yc_s = (s5_readout(sc_f, c_re[l, 0], c_im[l, 0])
                    + s5_readout(sc_b[:, ::-1], c_re[l, 1], c_im[l, 1]) + d_skip[l] * uc_s)
            sc_out = s5_glu(yc_s, w_glu[l], b_glu[l]).astype(x.dtype)
            fc_out = fourier_mix(uc_f, (ctx_len,))
            mix_c = merge_branches(fc_out, sc_out, zc_f, zc_s, w_f_proj[l], w_s_proj[l], w_out[l])
            ctx_x = layer_norm(DEEPNORM_ALPHA * ctx_x + mod_c[2][:, None] * mix_c, ln1_g[l], ln1_b[l])
        x = x_mid

        h2 = modulate(x, mod[3], mod[4]).reshape(bsz * seq, D_MODEL)
        moe_args = (w_group[l], b_group[l], w_expert[l], b_expert[l], w1[l], w3[l], w2[l])
        if last:
            y2 = hierarchical_moe(h2, *moe_args).reshape(bsz, seq, D_MODEL)
        else:
            h2c = modulate(ctx_x, mod_c[3], mod_c[4]).reshape(bsz * ctx_len, D_MODEL)
            y_all = hierarchical_moe(jnp.concatenate([h2, h2c], axis=0), *moe_args)
            y2 = y_all[:bsz * seq].reshape(bsz, seq, D_MODEL)
            y2c = y_all[bsz * seq:].reshape(bsz, ctx_len, D_MODEL)
            ctx_x = layer_norm(DEEPNORM_ALPHA * ctx_x + mod_c[5][:, None] * y2c, ln2_g[l], ln2_b[l])
        x = layer_norm(DEEPNORM_ALPHA * x + mod[5][:, None] * y2, ln2_g[l], ln2_b[l])
    return x
```

```python
import functools
import math

import numpy as np
import jax
import jax.numpy as jnp
from jax import lax
from jax.experimental import pallas as pl
from jax.experimental.pallas import tpu as pltpu

F32 = jnp.float32
BF16 = jnp.bfloat16

GRID_W = 64
F_GROUPS = 4
S_GROUP_DIM = 16
S_STATE = 64
N_EXPERT_GROUPS = 4
EXPERTS_PER_GROUP = 8
N_EXPERTS = N_EXPERT_GROUPS * EXPERTS_PER_GROUP
N_MOD = 6
LN_EPS = 1e-5

LANES = 128
S5_CHUNK = 64
OCTET = LANES // S_GROUP_DIM
MOE_BLOCK = 256
ROUTE_LANES = 128
VMEM_LIMIT = 56 * 1024 * 1024


def _params(sem, vmem=VMEM_LIMIT):
    return pltpu.CompilerParams(dimension_semantics=sem, vmem_limit_bytes=vmem)


def _ln_rows(x):
    mu = jnp.mean(x, axis=-1, keepdims=True)
    xc = x - mu
    var = jnp.mean(xc * xc, axis=-1, keepdims=True)
    return xc * lax.rsqrt(var + LN_EPS)


def _adaln_kernel(c_ref, w_ref, b_ref, o_ref):
    s = c_ref[...]
    s = s * jax.nn.sigmoid(s)
    o_ref[...] = jnp.dot(s.astype(BF16), w_ref[...].astype(BF16),
                         preferred_element_type=F32) + b_ref[...]


def _adaln(cc, w, b, tn=1024):
    m, d = cc.shape
    n = w.shape[1]
    return pl.pallas_call(
        _adaln_kernel,
        grid=(n // tn,),
        in_specs=[pl.BlockSpec((m, d), lambda j: (0, 0)),
                  pl.BlockSpec((d, tn), lambda j: (0, j)),
                  pl.BlockSpec((1, tn), lambda j: (0, j))],
        out_specs=pl.BlockSpec((m, tn), lambda j: (0, j)),
        out_shape=jax.ShapeDtypeStruct((m, n), F32),
        compiler_params=_params(("arbitrary",)),
        name="adaln",
    )(cc, w, b)


def _lnmm_kernel(x_ref, sh_ref, sc_ref, w_ref, o_ref, us_ref, h_scr, *, n_sig, us_tile, rc):
    j = pl.program_id(1)

    @pl.when(j == 0)
    def _():
        sh = sh_ref[0]
        sc1 = 1.0 + sc_ref[0]

        def body(r, carry):
            rows = pl.ds(pl.multiple_of(r * rc, rc), rc)
            h = _ln_rows(x_ref[rows, :]) * sc1 + sh
            h_scr[rows, :] = h.astype(BF16)
            return carry

        lax.fori_loop(0, x_ref.shape[0] // rc, body, 0)

    acc = jnp.dot(h_scr[...], w_ref[...], preferred_element_type=F32)

    @pl.when(j < n_sig)
    def _():
        o_ref[...] = jax.nn.sigmoid(acc).astype(o_ref.dtype)

    @pl.when(j >= n_sig)
    def _():
        o_ref[...] = acc.astype(o_ref.dtype)

    @pl.when(j == us_tile)
    def _():
        us_ref[...] = acc


def _ln_mod_matmul(x2, shift, scale, w, rows_per_mod, n_sig, us_tile, tm, tn):
    t, d = x2.shape
    n = w.shape[1]
    kern = functools.partial(_lnmm_kernel, n_sig=n_sig, us_tile=us_tile, rc=32)
    mod_map = lambda i, j: ((i * tm) // rows_per_mod, 0, 0)
    return pl.pallas_call(
        kern,
        grid=(t // tm, n // tn),
        in_specs=[pl.BlockSpec((tm, d), lambda i, j: (i, 0)),
                  pl.BlockSpec((1, 1, d), mod_map),
                  pl.BlockSpec((1, 1, d), mod_map),
                  pl.BlockSpec((d, tn), lambda i, j: (0, j))],
        out_specs=[pl.BlockSpec((tm, tn), lambda i, j: (i, j)),
                   pl.BlockSpec((tm, tn), lambda i, j: (i, 0))],
        out_shape=[jax.ShapeDtypeStruct((t, n), BF16),
                   jax.ShapeDtypeStruct((t, tn), F32)],
        scratch_shapes=[pltpu.VMEM((tm, d), BF16)],
        compiler_params=_params(("arbitrary", "arbitrary")),
        name="ln_mod_matmul",
    )(x2, shift, scale, w)


def _dft_cs(n):
    k = np.arange(n)
    ang = 2.0 * np.pi * np.outer(k, k) / n
    return np.cos(ang), np.sin(ang)


def _fourier_tables(rows, cols, gd):
    c1, s1 = _dft_cs(gd)
    half = gd // 2
    w1 = np.stack([np.concatenate([c1[:, h * half:(h + 1) * half],
                                   -s1[:, h * half:(h + 1) * half]], axis=1)
                   for h in range(2)]) / math.sqrt(gd)
    c2, s2 = _dft_cs(cols)
    rep = 256 // cols
    eye = np.eye(rep)
    l2 = np.concatenate([np.kron(eye, c2), np.kron(eye, s2)], axis=0) / math.sqrt(cols)
    c3, s3 = _dft_cs(rows)
    m3 = np.concatenate([c3, s3], axis=0)
    return (jnp.asarray(w1, BF16), jnp.asarray(l2, BF16), jnp.asarray(m3, BF16))


def _fourier_kernel(x_ref, w1_ref, l2_ref, m3_ref, o_ref, a_scr, b_scr, *, rows, cols, scale3):
    npos = rows * cols
    half = o_ref.shape[1]
    slab = l2_ref.shape[1]
    rc = 1024 if npos % 1024 == 0 else npos

    def s1(r, carry):
        rs = pl.ds(pl.multiple_of(r * rc, rc), rc)
        a_scr[rs, :] = jnp.dot(x_ref[rs, :], w1_ref[0],
                               preferred_element_type=F32).astype(BF16)
        return carry

    lax.fori_loop(0, npos // rc, s1, 0)

    def s2(s, carry):
        rs = pl.ds(pl.multiple_of(s * slab, slab), slab)
        g = jnp.dot(l2_ref[...], a_scr[rs, :], preferred_element_type=F32)
        b_scr[0, rs, :] = g[0:slab, 0:half] + g[slab:, half:]
        b_scr[1, rs, :] = g[0:slab, half:] - g[slab:, 0:half]
        return carry

    lax.fori_loop(0, npos // slab, s2, 0)

    for c in range(cols):
        bc = jnp.concatenate([b_scr[0, pl.ds(c, rows, stride=cols), :],
                              b_scr[1, pl.ds(c, rows, stride=cols), :]], axis=1).astype(BF16)
        h = jnp.dot(m3_ref[...], bc, preferred_element_type=F32)
        o_ref[pl.ds(c, rows, stride=cols), :] = (h[0:rows, 0:half] + h[rows:, half:]) * scale3


def _fourier_mix(u_all, col0, batch, rows, cols, gd):
    npos = rows * cols
    half = gd // 2
    w1, l2, m3 = _fourier_tables(rows, cols, gd)
    kern = functools.partial(_fourier_kernel, rows=rows, cols=cols, scale3=1.0 / math.sqrt(rows))
    cb0 = col0 // gd
    return pl.pallas_call(
        kern,
        grid=(batch, F_GROUPS, 2),
        in_specs=[pl.BlockSpec((npos, gd), lambda b, g, h: (b, cb0 + g)),
                  pl.BlockSpec((1, gd, gd), lambda b, g, h: (h, 0, 0)),
                  pl.BlockSpec(l2.shape, lambda b, g, h: (0, 0)),
                  pl.BlockSpec(m3.shape, lambda b, g, h: (0, 0))],
        out_specs=pl.BlockSpec((npos, half), lambda b, g, h: (b, g * 2 + h)),
        out_shape=jax.ShapeDtypeStruct((batch * npos, F_GROUPS * gd), F32),
        scratch_shapes=[pltpu.VMEM((npos, gd), BF16), pltpu.VMEM((2, npos, half), F32)],
        compiler_params=_params(("arbitrary", "arbitrary", "arbitrary")),
        name="fourier_mix",
    )(u_all, w1, l2, m3)


def _octet_perm():
    n = OCTET * LANES
    p = np.zeros((n, n), np.float32)
    for k in range(OCTET):
        for g in range(OCTET):
            for h in range(S_GROUP_DIM):
                p[k * LANES + g * S_GROUP_DIM + h, g * LANES + k * S_GROUP_DIM + h] = 1.0
    return jnp.asarray(p, BF16)


def _relayout_in_kernel(x_ref, p_ref, o_ref, *, nchunk):
    io = pl.program_id(1)
    pieces = []
    for k in range(OCTET):
        pieces.append(x_ref[pl.ds(io * OCTET + k, nchunk, stride=S5_CHUNK), :].astype(BF16))
    a = jnp.concatenate(pieces, axis=1)
    b = jnp.dot(a, p_ref[...], preferred_element_type=F32).astype(BF16)
    for g in range(OCTET):
        o_ref[g] = b[:, g * LANES:(g + 1) * LANES]


def _relayout_in(us, perm):
    t, sw = us.shape
    nchunk = t // S5_CHUNK
    ngroups = sw // S_GROUP_DIM
    kern = functools.partial(_relayout_in_kernel, nchunk=nchunk)
    return pl.pallas_call(
        kern,
        grid=(sw // LANES, S5_CHUNK // OCTET),
        in_specs=[pl.BlockSpec((t, LANES), lambda o, io: (0, o)),
                  pl.BlockSpec(perm.shape, lambda o, io: (0, 0))],
        out_specs=pl.BlockSpec((OCTET, nchunk, LANES), lambda o, io: (o, 0, io)),
        out_shape=jax.ShapeDtypeStruct((ngroups, nchunk, S5_CHUNK * S_GROUP_DIM), BF16),
        compiler_params=_params(("arbitrary", "arbitrary")),
        name="s5_relayout_in",
    )(us, perm)


def _relayout_out_kernel(y_ref, p_ref, o_ref, *, nchunk):
    io = pl.program_id(1)
    b = jnp.concatenate([y_ref[g] for g in range(OCTET)], axis=1)
    a = jnp.dot(b, p_ref[...], preferred_element_type=F32)
    for k in range(OCTET):
        o_ref[pl.ds(io * OCTET + k, nchunk, stride=S5_CHUNK), :] = a[:, k * LANES:(k + 1) * LANES]


def _relayout_out(y, perm):
    ngroups, nchunk, _ = y.shape
    t = nchunk * S5_CHUNK
    sw = ngroups * S_GROUP_DIM
    kern = functools.partial(_relayout_out_kernel, nchunk=nchunk)
    return pl.pallas_call(
        kern,
        grid=(sw // LANES, S5_CHUNK // OCTET),
        in_specs=[pl.BlockSpec((OCTET, nchunk, LANES), lambda o, io: (o, 0, io)),
                  pl.BlockSpec(perm.shape, lambda o, io: (0, 0))],
        out_specs=pl.BlockSpec((t, LANES), lambda o, io: (0, o)),
        out_shape=jax.ShapeDtypeStruct((t, sw), F32),
        compiler_params=_params(("arbitrary", "arbitrary")),
        name="s5_relayout_out",
    )(y, perm)


def _s5_tables(lam_re, lam_im, log_dt, b_re, b_im, c_re, c_im, d_skip):
    lc = S5_CHUNK
    ii = jnp.arange(lc, dtype=F32)

    def direction(d):
        lam = lax.complex(lam_re[d].astype(F32), lam_im[d].astype(F32))
        dt = jnp.exp(log_dt[d].astype(F32))[:, None]
        z = lam * dt
        lam_bar = jnp.exp(z)
        b_bar = ((lam_bar - 1.0) / lam)[:, :, None] * lax.complex(b_re[d].astype(F32),
                                                                  b_im[d].astype(F32))
        cc = lax.complex(c_re[d].astype(F32), c_im[d].astype(F32))
        return z, b_bar, cc

    zf, bf, cf = direction(0)
    zb, bb, cb = direction(1)

    def pw(z, e):
        return jnp.exp(z[:, None, :] * e[None, :, None])

    def rows_of(bm):
        g, l, p, h = bm.shape
        return jnp.transpose(bm, (0, 1, 3, 2)).reshape(g, l * h, p)

    def cols_of(cm):
        g, l, h, p = cm.shape
        return jnp.transpose(cm, (0, 3, 1, 2)).reshape(g, p, l * h)

    bmf = rows_of(pw(zf, -ii)[:, :, :, None] * bf[:, None])
    cmf = cols_of(pw(zf, ii)[:, :, None, :] * cf[:, None])
    bmb = rows_of(pw(zb, ii)[:, :, :, None] * bb[:, None])
    cmb = cols_of(pw(zb, -ii)[:, :, None, :] * cb[:, None])
    uf = jnp.concatenate([bmf.real, -bmf.imag], axis=2)
    vf = jnp.concatenate([cmf.real, cmf.imag], axis=1)
    ub = jnp.concatenate([bmb.real, -bmb.imag], axis=2)
    vb = jnp.concatenate([cmb.real, cmb.imag], axis=1)
    pf = rows_of(pw(zf, (lc - 1) - ii)[:, :, :, None] * bf[:, None])
    pb = rows_of(pw(zb, ii)[:, :, :, None] * bb[:, None])
    pmat = jnp.concatenate([pf.real, pb.real, pf.imag, pb.imag], axis=2)
    qf = cols_of(pw(zf, ii + 1.0)[:, :, None, :] * cf[:, None])
    qb = cols_of(pw(zb, lc - ii)[:, :, None, :] * cb[:, None])
    qmat = jnp.concatenate([qf.real, qb.real, -qf.imag, -qb.imag], axis=1)
    af = jnp.exp(zf * lc)
    ab = jnp.exp(zb * lc)
    decay = jnp.stack([jnp.concatenate([af.real, ab.real], axis=1),
                       jnp.concatenate([af.imag, ab.imag], axis=1)], axis=1)
    dvec = jnp.tile(d_skip.astype(F32), (1, lc))[:, None, :]
    bf16 = lambda a: a.astype(BF16)
    return bf16(uf), bf16(vf), bf16(ub), bf16(vb), bf16(pmat), bf16(qmat), decay, dvec


def _s5_kernel(z_ref, zc_ref, uf_ref, vf_ref, ub_ref, vb_ref, p_ref, q_ref, dec_ref, dv_ref,
               y_ref, e_scr, ec_scr, srf_scr, srb_scr, sif_scr, sib_scr, *, batch, nlat, nctx):
    p2 = 2 * S_STATE
    z = z_ref[0]
    e = jnp.dot(z, p_ref[0], preferred_element_type=F32)
    ec = jnp.dot(zc_ref[0], p_ref[0], preferred_element_type=F32)
    for col in range(2):
        e_scr[col] = e[:, col * p2:(col + 1) * p2]
        ec_scr[col] = ec[:, col * p2:(col + 1) * p2]

    ar = dec_ref[0, 0:1, :]
    ai = dec_ref[0, 1:2, :]
    lane = lax.broadcasted_iota(jnp.int32, (batch, p2), 1)
    is_f = lane < S_STATE

    def advance(sr, si, er, ei):
        return ar * sr - ai * si + er, ar * si + ai * sr + ei

    def pick(ref, nf, nb, n, col):
        f = ref[col, pl.ds(nf, batch, stride=n), :]
        b = ref[col, pl.ds(nb, batch, stride=n), :]
        return jnp.where(is_f, f, b)

    sr = jnp.zeros((batch, p2), F32)
    si = jnp.zeros((batch, p2), F32)
    for k in range(nctx):
        nf, nb = k, nctx - 1 - k
        sr, si = advance(sr, si, pick(ec_scr, nf, nb, nctx, 0), pick(ec_scr, nf, nb, nctx, 1))
    for k in range(nlat):
        nf, nb = k, nlat - 1 - k
        rf = pl.ds(nf, batch, stride=nlat)
        rb = pl.ds(nb, batch, stride=nlat)
        srf_scr[rf, :] = sr
        srb_scr[rb, :] = sr
        sif_scr[rf, :] = si
        sib_scr[rb, :] = si
        sr, si = advance(sr, si, pick(e_scr, nf, nb, nlat, 0), pick(e_scr, nf, nb, nlat, 1))
    lane_all = lax.broadcasted_iota(jnp.int32, srf_scr.shape, 1)
    s_re = jnp.where(lane_all < S_STATE, srf_scr[...], srb_scr[...])
    s_im = jnp.where(lane_all < S_STATE, sif_scr[...], sib_scr[...])
    s_all = jnp.concatenate([s_re, s_im], axis=1).astype(BF16)

    n = z.shape[1]
    tf = jnp.dot(uf_ref[0], vf_ref[0], preferred_element_type=F32)
    tb = jnp.dot(ub_ref[0], vb_ref[0], preferred_element_type=F32)
    ri = lax.broadcasted_iota(jnp.int32, (n, n), 0)
    ci = lax.broadcasted_iota(jnp.int32, (n, n), 1)
    rpos = ri // S_GROUP_DIM
    cpos = ci // S_GROUP_DIM
    tmat = (jnp.where(rpos <= cpos, tf, 0.0) + jnp.where(rpos >= cpos, tb, 0.0)
            + jnp.where(ri == ci, dv_ref[0], 0.0))
    y = jnp.dot(z, tmat.astype(BF16), preferred_element_type=F32)
    y = y + jnp.dot(s_all, q_ref[0], preferred_element_type=F32)
    y_ref[0] = y.astype(y_ref.dtype)


def _s5_core(zl, zc, tables, batch):
    uf, vf, ub, vb, pmat, qmat, decay, dvec = tables
    g, rl, n = zl.shape
    rcx = zc.shape[1]
    nlat, nctx = rl // batch, rcx // batch
    kern = functools.partial(_s5_kernel, batch=batch, nlat=nlat, nctx=nctx)
    gmap = lambda i: (i, 0, 0)
    spec = lambda a: pl.BlockSpec((1,) + a.shape[1:], gmap)
    return pl.pallas_call(
        kern,
        grid=(g,),
        in_specs=[spec(zl), spec(zc), spec(uf), spec(vf), spec(ub), spec(vb), spec(pmat),
                  spec(qmat), spec(decay), spec(dvec)],
        out_specs=pl.BlockSpec((1, rl, n), gmap),
        out_shape=jax.ShapeDtypeStruct((g, rl, n), BF16),
        scratch_shapes=[pltpu.VMEM((2, rl, 2 * S_STATE), F32),
                        pltpu.VMEM((2, rcx, 2 * S_STATE), F32)]
                       + [pltpu.VMEM((rl, 2 * S_STATE), F32)] * 4,
        compiler_params=_params(("arbitrary",)),
        name="s5_core",
    )(zl, zc, uf, vf, ub, vb, pmat, qmat, decay, dvec)


def _merge_kernel(fo_ref, ys_ref, zf_ref, zs_ref, x_ref, g1_ref, sh2_ref, sc2_ref,
                  wglu_ref, bglu_ref, wf_ref, ws_ref, wo_ref, l1g_ref, l1b_ref, wr_ref, br_ref,
                  xm_ref, h2_ref, lg_ref, mix_scr, h2_scr, *, alpha, rc):
    y = jax.nn.gelu(ys_ref[...], approximate=True)
    glu = jnp.dot(y.astype(BF16), wglu_ref[...], preferred_element_type=F32) + bglu_ref[...]
    s_out = y * jax.nn.sigmoid(glu)
    a = jnp.dot(fo_ref[...].astype(BF16), wf_ref[...], preferred_element_type=F32)
    b = jnp.dot(s_out.astype(BF16), ws_ref[...], preferred_element_type=F32)
    m = zf_ref[...].astype(F32) * a + zs_ref[...].astype(F32) * b
    mix_scr[...] = jnp.dot(m.astype(BF16), wo_ref[...], preferred_element_type=F32)

    g1 = g1_ref[0]
    sh2 = sh2_ref[0]
    sc2 = 1.0 + sc2_ref[0]
    l1g = l1g_ref[...]
    l1b = l1b_ref[...]

    def body(r, carry):
        rows = pl.ds(pl.multiple_of(r * rc, rc), rc)
        xm = _ln_rows(alpha * x_ref[rows, :] + g1 * mix_scr[rows, :]) * l1g + l1b
        xm_ref[rows, :] = xm
        h2 = _ln_rows(xm) * sc2 + sh2
        h2_ref[rows, :] = h2
        h2_scr[rows, :] = h2.astype(BF16)
        return carry

    lax.fori_loop(0, x_ref.shape[0] // rc, body, 0)
    lg_ref[...] = jnp.dot(h2_scr[...], wr_ref[...], preferred_element_type=F32) + br_ref[...]


def _merge(f_out, y_s, main, x2, gate1, shift2, scale2, wglu, bglu, wf, ws, wo, l1g, l1b, wr, br,
           seq, alpha, tm=256):
    t, d = x2.shape
    fw, sw = f_out.shape[1], y_s.shape[1]
    kern = functools.partial(_merge_kernel, alpha=alpha, rc=32)
    row = lambda w: pl.BlockSpec((tm, w), lambda i: (i, 0))
    mod = pl.BlockSpec((1, 1, d), lambda i: ((i * tm) // seq, 0, 0))
    full = lambda a: pl.BlockSpec(a.shape, lambda i: (0,) * a.ndim)
    return pl.pallas_call(
        kern,
        grid=(t // tm,),
        in_specs=[row(fw), row(sw),
                  pl.BlockSpec((tm, d), lambda i: (i, 0)),
                  pl.BlockSpec((tm, d), lambda i: (i, 1)),
                  row(d), mod, mod, mod,
                  full(wglu), full(bglu), full(wf), full(ws), full(wo), full(l1g), full(l1b),
                  full(wr), full(br)],
        out_specs=[row(d), row(d), row(ROUTE_LANES)],
        out_shape=[jax.ShapeDtypeStruct((t, d), F32), jax.ShapeDtypeStruct((t, d), F32),
                   jax.ShapeDtypeStruct((t, ROUTE_LANES), F32)],
        scratch_shapes=[pltpu.VMEM((tm, d), F32), pltpu.VMEM((tm, d), BF16)],
        compiler_params=_params(("arbitrary",)),
        name="merge",
    )(f_out, y_s, main, main, x2, gate1, shift2, scale2, wglu, bglu, wf, ws, wo, l1g, l1b, wr, br)


def _route_kernel(lg_ref, tri_ref, info_ref, cnt_ref, run_scr):
    i = pl.program_id(0)

    @pl.when(i == 0)
    def _():
        run_scr[...] = jnp.zeros_like(run_scr)

    lg = lg_ref[...]
    tm = lg.shape[0]
    lane = lax.broadcasted_iota(jnp.int32, lg.shape, 1)
    neg = jnp.float32(-1e30)
    big = jnp.int32(ROUTE_LANES)
    ng, epg = N_EXPERT_GROUPS, EXPERTS_PER_GROUP

    is_g = lane < ng
    gl = jnp.where(is_g, lg, neg)
    gm = jnp.max(gl, axis=-1, keepdims=True)
    gidx = jnp.min(jnp.where(gl == gm, lane, big), axis=-1, keepdims=True)
    gsum = jnp.sum(jnp.where(is_g, jnp.exp(gl - gm), 0.0), axis=-1, keepdims=True)
    g_top = 1.0 / gsum

    lo = ng + gidx * epg
    el = jnp.where((lane >= lo) & (lane < lo + epg), lg, neg)
    m1 = jnp.max(el, axis=-1, keepdims=True)
    i1 = jnp.min(jnp.where(el == m1, lane, big), axis=-1, keepdims=True)
    el2 = jnp.where(lane == i1, neg, el)
    m2 = jnp.max(el2, axis=-1, keepdims=True)
    i2 = jnp.min(jnp.where(el2 == m2, lane, big), axis=-1, keepdims=True)
    tt = jnp.exp(m2 - m1)
    w0 = g_top / (1.0 + tt)
    w1 = g_top * tt / (1.0 + tt)
    e0 = i1 - ng
    e1 = i2 - ng

    oh = (lane == e0) | (lane == e1)
    pre = jnp.dot(tri_ref[...], jnp.where(oh, 1.0, 0.0).astype(BF16),
                  preferred_element_type=F32) + run_scr[...]
    r0 = jnp.sum(jnp.where(lane == e0, pre, 0.0), axis=-1, keepdims=True)
    r1 = jnp.sum(jnp.where(lane == e1, pre, 0.0), axis=-1, keepdims=True)
    run_scr[...] = run_scr[...] + jnp.sum(jnp.where(oh, 1.0, 0.0), axis=0, keepdims=True)
    cnt_ref[...] = run_scr[...]

    vals = (e0.astype(F32), e1.astype(F32), w0, w1, r0, r1)
    info = jnp.zeros(lg.shape, F32)
    for k, v in enumerate(vals):
        info = jnp.where(lane == k, v, info)
    info_ref[...] = info


def _route(logits, tm=512):
    t = logits.shape[0]
    tri = jnp.asarray(np.tril(np.ones((tm, tm), np.float32), -1), BF16)
    return pl.pallas_call(
        _route_kernel,
        grid=(t // tm,),
        in_specs=[pl.BlockSpec((tm, ROUTE_LANES), lambda i: (i, 0)),
                  pl.BlockSpec((tm, tm), lambda i: (0, 0))],
        out_specs=[pl.BlockSpec((tm, ROUTE_LANES), lambda i: (i, 0)),
                   pl.BlockSpec((1, ROUTE_LANES), lambda i: (0, 0))],
        out_shape=[jax.ShapeDtypeStruct((t, ROUTE_LANES), F32),
                   jax.ShapeDtypeStruct((1, ROUTE_LANES), F32)],
        scratch_shapes=[pltpu.VMEM((1, ROUTE_LANES), F32)],
        compiler_params=_params(("arbitrary",)),
        name="route",
    )(logits, tri)


def _row_copy(src_hbm, idx_ref, n, dst, sem):
    for r in range(n):
        pltpu.make_async_copy(src_hbm.at[pl.ds(idx_ref[0, 0, r], 1)],
                              dst.at[pl.ds(r, 1)], sem).start()


def _row_wait(src_hbm, n, dst, sem):
    for r in range(n):
        pltpu.make_async_copy(src_hbm.at[pl.ds(0, 1)], dst.at[pl.ds(r, 1)], sem).wait()


def _expert_kernel(bexp_ref, nu_ref, idx0_ref, idx1_ref, h_hbm, w1_ref, w3_ref, w2_ref,
                   y_ref, xbuf, sem):
    j = pl.program_id(0)
    nu = nu_ref[0]
    blk = xbuf.shape[1]
    slot = j % 2
    nslot = (j + 1) % 2

    @pl.when(j == 0)
    def _():
        _row_copy(h_hbm, idx0_ref, blk, xbuf.at[0], sem.at[0])

    @pl.when(j + 1 < nu)
    def _():
        _row_copy(h_hbm, idx1_ref, blk, xbuf.at[nslot], sem.at[nslot])

    @pl.when(j < nu)
    def _():
        _row_wait(h_hbm, blk, xbuf.at[slot], sem.at[slot])
        x = xbuf[slot].astype(BF16)
        a = jnp.dot(x, w1_ref[0], preferred_element_type=F32)
        b = jnp.dot(x, w3_ref[0], preferred_element_type=F32)
        hm = (a * jax.nn.sigmoid(a)) * b
        y_ref[...] = jnp.dot(hm.astype(BF16), w2_ref[0], preferred_element_type=F32)

    @pl.when(j >= nu)
    def _():
        y_ref[...] = jnp.zeros_like(y_ref)


def _experts(h2, row_tok, block_exp, n_used, w1, w3, w2):
    t, d = h2.shape
    nb = block_exp.shape[0]
    blk = MOE_BLOCK
    f = w1.shape[2]
    idx = row_tok.reshape(nb, 1, blk)
    smem = lambda m: pl.BlockSpec((1, 1, blk), m, memory_space=pltpu.SMEM)
    grid_spec = pltpu.PrefetchScalarGridSpec(
        num_scalar_prefetch=2,
        grid=(nb,),
        in_specs=[smem(lambda j, be, nu: (j, 0, 0)),
                  smem(lambda j, be, nu: (jnp.minimum(j + 1, nb - 1), 0, 0)),
                  pl.BlockSpec(memory_space=pl.ANY),
                  pl.BlockSpec((1, d, f), lambda j, be, nu: (be[j], 0, 0)),
                  pl.BlockSpec((1, d, f), lambda j, be, nu: (be[j], 0, 0)),
                  pl.BlockSpec((1, f, d), lambda j, be, nu: (be[j], 0, 0))],
        out_specs=pl.BlockSpec((blk, d), lambda j, be, nu: (j, 0)),
        scratch_shapes=[pltpu.VMEM((2, blk, d), F32), pltpu.SemaphoreType.DMA((2,))],
    )
    return pl.pallas_call(
        _expert_kernel,
        grid_spec=grid_spec,
        out_shape=jax.ShapeDtypeStruct((nb * blk, d), F32),
        compiler_params=_params(("arbitrary",)),
        name="experts",
    )(block_exp, n_used, idx, idx, h2, w1, w3, w2)


def _combine_kernel(idx0_ref, idx1_ref, yb_hbm, xm_ref, w_ref, g2_ref, lg_ref, lb_ref,
                    o_ref, ybuf, sem, *, alpha, rc):
    i = pl.program_id(0)
    n = pl.num_programs(0)
    tm = xm_ref.shape[0]
    slot = i % 2
    nslot = (i + 1) % 2

    @pl.when(i == 0)
    def _():
        _row_copy(yb_hbm, idx0_ref, 2 * tm, ybuf.at[0], sem.at[0])

    @pl.when(i + 1 < n)
    def _():
        _row_copy(yb_hbm, idx1_ref, 2 * tm, ybuf.at[nslot], sem.at[nslot])

    _row_wait(yb_hbm, 2 * tm, ybuf.at[slot], sem.at[slot])
    g2 = g2_ref[0]
    lg = lg_ref[...]
    lb = lb_ref[...]

    def body(r, carry):
        rows = pl.ds(pl.multiple_of(r * rc, rc), rc)
        rows1 = pl.ds(pl.multiple_of(tm + r * rc, rc), rc)
        w = w_ref[rows, :]
        y2 = w[:, 0:1] * ybuf[slot, rows, :] + w[:, 1:2] * ybuf[slot, rows1, :]
        o_ref[rows, :] = _ln_rows(alpha * xm_ref[rows, :] + g2 * y2) * lg + lb
        return carry

    lax.fori_loop(0, tm // rc, body, 0)


def _combine(yb, dest, x_mid, wts, gate2, l2g, l2b, seq, alpha, tm=256):
    t, d = x_mid.shape
    nt = t // tm
    idx = jnp.transpose(dest.reshape(nt, tm, 2), (0, 2, 1)).reshape(nt, 1, 2 * tm)
    kern = functools.partial(_combine_kernel, alpha=alpha, rc=32)
    smem = lambda m: pl.BlockSpec((1, 1, 2 * tm), m, memory_space=pltpu.SMEM)
    full = lambda a: pl.BlockSpec(a.shape, lambda i: (0,) * a.ndim)
    return pl.pallas_call(
        kern,
        grid=(nt,),
        in_specs=[smem(lambda i: (i, 0, 0)),
                  smem(lambda i: (jnp.minimum(i + 1, nt - 1), 0, 0)),
                  pl.BlockSpec(memory_space=pl.ANY),
                  pl.BlockSpec((tm, d), lambda i: (i, 0)),
                  pl.BlockSpec((tm, 2), lambda i: (i, 0)),
                  pl.BlockSpec((1, 1, d), lambda i: ((i * tm) // seq, 0, 0)),
                  full(l2g), full(l2b)],
        out_specs=pl.BlockSpec((tm, d), lambda i: (i, 0)),
        out_shape=jax.ShapeDtypeStruct((t, d), F32),
        scratch_shapes=[pltpu.VMEM((2, 2 * tm, d), F32), pltpu.SemaphoreType.DMA((2,))],
        compiler_params=_params(("arbitrary",)),
        name="combine",
    )(idx, idx, yb, x_mid, wts, gate2, l2g, l2b)


def kernel(x, c, ctx, c_ctx, w_ada, b_ada, w_in, w_f_proj, lam_re, lam_im, log_dt, b_re, b_im,
           c_re, c_im, d_skip, w_glu, b_glu, w_s_proj, w_out, ln1_g, ln1_b, w_group, b_group,
           w_expert, b_expert, w1, w3, w2, ln2_g, ln2_b):
    depth = w_ada.shape[0]
    assert depth == 1, "single-layer block"
    bsz, seq, d = x.shape
    ctx_len = ctx.shape[1]
    t = bsz * seq
    fw = w_f_proj.shape[1]
    sw = w_s_proj.shape[1]
    gd = fw // F_GROUPS
    rows = seq // GRID_W
    alpha = (2.0 * depth) ** 0.25
    row2 = lambda v: v.reshape(1, -1).astype(F32)

    cc = jnp.concatenate([c, c_ctx[None], jnp.zeros((8 - bsz - 1, d), F32)], axis=0)
    mod = _adaln(cc, w_ada[0], b_ada[0].reshape(1, -1))
    mods = [mod[:, k * d:(k + 1) * d] for k in range(N_MOD)]
    per_b = lambda m: m[:bsz].reshape(bsz, 1, d)
    ctx_row = lambda m: m[bsz:bsz + 1].reshape(1, 1, d)

    wi = w_in[0]
    wi = jnp.concatenate([wi[:, fw + sw:], wi[:, :fw + sw]], axis=1).astype(BF16)
    tn = sw
    n_sig = (2 * d) // tn
    us_tile = (2 * d + fw) // tn
    main, us = _ln_mod_matmul(x.reshape(t, d), per_b(mods[0]), per_b(mods[1]), wi,
                              rows_per_mod=seq, n_sig=n_sig, us_tile=us_tile, tm=1024, tn=tn)
    w_us = w_in[0][:, fw:fw + sw].astype(BF16)
    _, us_ctx = _ln_mod_matmul(ctx.reshape(bsz * ctx_len, d), ctx_row(mods[0]), ctx_row(mods[1]),
                               w_us, rows_per_mod=bsz * ctx_len, n_sig=0, us_tile=0,
                               tm=ctx_len, tn=tn)

    f_out = _fourier_mix(main, 2 * d, bsz, rows, GRID_W, gd)

    perm = _octet_perm()
    tables = _s5_tables(lam_re[0], lam_im[0], log_dt[0], b_re[0], b_im[0], c_re[0], c_im[0],
                        d_skip[0])
    zl = _relayout_in(us, perm)
    zc = _relayout_in(us_ctx, perm)
    y_chunks = _s5_core(zl, zc, tables, bsz)
    y_s = _relayout_out(y_chunks, perm)

    wr = jnp.concatenate([w_group[0], w_expert[0],
                          jnp.zeros((d, ROUTE_LANES - N_EXPERT_GROUPS - N_EXPERTS), F32)], axis=1)
    br = jnp.concatenate([b_group[0], b_expert[0],
                          jnp.zeros((ROUTE_LANES - N_EXPERT_GROUPS - N_EXPERTS,), F32)])
    x_mid, h2, logits = _merge(
        f_out, y_s, main, x.reshape(t, d), per_b(mods[2]), per_b(mods[3]), per_b(mods[4]),
        w_glu[0].astype(BF16), row2(b_glu[0]), w_f_proj[0].astype(BF16), w_s_proj[0].astype(BF16),
        w_out[0].astype(BF16), row2(ln1_g[0]), row2(ln1_b[0]), wr.astype(BF16), row2(br),
        seq, alpha)

    info, cnt = _route(logits)
    experts = info[:, 0:2].astype(jnp.int32)
    wts = info[:, 2:4]
    rank = info[:, 4:6].astype(jnp.int32)
    counts = cnt[0, :N_EXPERTS].astype(jnp.int32)
    blk = MOE_BLOCK
    n_assign = 2 * t
    nb = -(-(n_assign + N_EXPERTS * (blk - 1)) // blk)
    padded = (counts + blk - 1) // blk * blk
    pend = jnp.cumsum(padded)
    pstart = pend - padded
    dest = pstart[experts] + rank
    n_used = (pend[-1] // blk).astype(jnp.int32).reshape(1)
    block_exp = jnp.minimum(jnp.searchsorted(pend, jnp.arange(nb, dtype=jnp.int32) * blk,
                                             side="right"), N_EXPERTS - 1).astype(jnp.int32)
    tok = jnp.repeat(jnp.arange(t, dtype=jnp.int32), 2)
    row_tok = jnp.zeros((nb * blk,), jnp.int32).at[dest.reshape(-1)].set(tok)

    yb = _experts(h2, row_tok, block_exp, n_used, w1[0].astype(BF16), w3[0].astype(BF16),
                  w2[0].astype(BF16))
    out = _combine(yb, dest, x_mid, wts, per_b(mods[5]), row2(ln2_g[0]), row2(ln2_b[0]), seq, alpha)
    return out.reshape(bsz, seq, d)
```

```python
import functools
import math

import numpy as np
import jax
import jax.numpy as jnp
from jax import lax
from jax.experimental import pallas as pl
from jax.experimental.pallas import tpu as pltpu

F32 = jnp.float32
BF16 = jnp.bfloat16

GRID_W = 64
F_GROUPS = 4
S_GROUP_DIM = 16
S_STATE = 64
N_EXPERT_GROUPS = 4
EXPERTS_PER_GROUP = 8
N_EXPERTS = N_EXPERT_GROUPS * EXPERTS_PER_GROUP
N_MOD = 6
LN_EPS = 1e-5

LANES = 128
S5_CHUNK = 64
OCTET = LANES // S_GROUP_DIM
MOE_BLOCK = 256
ROUTE_LANES = 128
VMEM_LIMIT = 56 * 1024 * 1024


def _params(sem, vmem=VMEM_LIMIT):
    return pltpu.CompilerParams(dimension_semantics=sem, vmem_limit_bytes=vmem)


def _ln_rows(x):
    mu = jnp.mean(x, axis=-1, keepdims=True)
    xc = x - mu
    var = jnp.mean(xc * xc, axis=-1, keepdims=True)
    return xc * lax.rsqrt(var + LN_EPS)


def _adaln_kernel(c_ref, w_ref, b_ref, o_ref):
    s = c_ref[...]
    s = s * jax.nn.sigmoid(s)
    o_ref[...] = jnp.dot(s.astype(BF16), w_ref[...].astype(BF16),
                         preferred_element_type=F32) + b_ref[...]


def _adaln(cc, w, b, tn=1024):
    m, d = cc.shape
    n = w.shape[1]
    return pl.pallas_call(
        _adaln_kernel,
        grid=(n // tn,),
        in_specs=[pl.BlockSpec((m, d), lambda j: (0, 0)),
                  pl.BlockSpec((d, tn), lambda j: (0, j)),
                  pl.BlockSpec((1, tn), lambda j: (0, j))],
        out_specs=pl.BlockSpec((m, tn), lambda j: (0, j)),
        out_shape=jax.ShapeDtypeStruct((m, n), F32),
        compiler_params=_params(("arbitrary",)),
        name="adaln",
    )(cc, w, b)


def _ln_mod_rows(x_ref, rows, sc1, sh):
    mu = jnp.mean(x_ref[rows, :], axis=-1, keepdims=True)
    xc = x_ref[rows, :] - mu
    var = jnp.mean(xc * xc, axis=-1, keepdims=True)
    return (x_ref[rows, :] - mu) * lax.rsqrt(var + LN_EPS) * sc1 + sh


def _lnmm_kernel(x0_ref, sh0_ref, sc0_ref, xn_ref, shn_ref, scn_ref, w_ref, o_ref, us_ref, h_scr,
                 *, n_sig, nrc, mchunks):
    i = pl.program_id(0)
    j = pl.program_id(1)
    tm = xn_ref.shape[0]
    rcs = tm // nrc
    cur = i % 2
    nxt = (i + 1) % 2

    @pl.when((i == 0) & (j == 0))
    def _():
        def body(r, carry):
            rows = pl.ds(pl.multiple_of(r * rcs, rcs), rcs)
            h_scr[0, rows, :] = _ln_mod_rows(x0_ref, rows, 1.0 + sc0_ref[0],
                                             sh0_ref[0]).astype(BF16)
            return carry

        lax.fori_loop(0, nrc, body, 0)

    c = jnp.minimum(j, nrc - 1)
    rows = pl.ds(pl.multiple_of(c * rcs, rcs), rcs)
    h_scr[nxt, rows, :] = _ln_mod_rows(xn_ref, rows, 1.0 + scn_ref[0], shn_ref[0]).astype(BF16)

    tmc = tm // mchunks
    for m in range(mchunks):
        rs = pl.ds(m * tmc, tmc)
        acc = jnp.dot(h_scr[cur, rs, :], w_ref[...], preferred_element_type=F32)
        o_ref[rs, :] = jnp.where(j < n_sig, jax.nn.sigmoid(acc), acc).astype(o_ref.dtype)
        us_ref[rs, :] = acc


def _ln_mod_matmul(x2, shift, scale, w, rows_per_mod, n_sig, tm, tn):
    t, d = x2.shape
    n = w.shape[1]
    nt, nj = t // tm, n // tn
    nrc = min(8, nj)
    kern = functools.partial(_lnmm_kernel, n_sig=n_sig, nrc=nrc, mchunks=max(1, tm // 256))
    nxt = lambda i: jnp.minimum(i + 1, nt - 1)
    once = pl.Buffered(1)
    mod0 = pl.BlockSpec((1, 1, d), lambda i, j: (0, 0, 0), pipeline_mode=once)
    modn = pl.BlockSpec((1, 1, d), lambda i, j: ((nxt(i) * tm) // rows_per_mod, 0, 0))
    return pl.pallas_call(
        kern,
        grid=(nt, nj),
        in_specs=[pl.BlockSpec((tm, d), lambda i, j: (0, 0), pipeline_mode=once), mod0, mod0,
                  pl.BlockSpec((tm, d), lambda i, j: (nxt(i), 0)), modn, modn,
                  pl.BlockSpec((d, tn), lambda i, j: (0, j))],
        out_specs=[pl.BlockSpec((tm, tn), lambda i, j: (i, j)),
                   pl.BlockSpec((tm, tn), lambda i, j: (i, 0))],
        out_shape=[jax.ShapeDtypeStruct((t, n), BF16),
                   jax.ShapeDtypeStruct((t, tn), F32)],
        scratch_shapes=[pltpu.VMEM((2, tm, d), BF16)],
        compiler_params=_params(("arbitrary", "arbitrary")),
        name="ln_mod_matmul",
    )(x2, shift, scale, x2, shift, scale, w)


def _dft_cs(n):
    k = np.arange(n)
    ang = 2.0 * np.pi * np.outer(k, k) / n
    return np.cos(ang), np.sin(ang)


def _fourier_tables(rows, cols, gd):
    c1, s1 = _dft_cs(gd)
    half = gd // 2
    w1 = np.stack([np.concatenate([c1[:, h * half:(h + 1) * half],
                                   -s1[:, h * half:(h + 1) * half]], axis=1)
                   for h in range(2)]) / math.sqrt(gd)
    c2, s2 = _dft_cs(cols)
    rep = 256 // cols
    eye = np.eye(rep)
    l2 = np.concatenate([np.kron(eye, c2), np.kron(eye, s2)], axis=0) / math.sqrt(cols)
    c3, s3 = _dft_cs(rows)
    m3 = np.concatenate([c3, s3], axis=0)
    return (jnp.asarray(w1, BF16), jnp.asarray(l2, BF16), jnp.asarray(m3, BF16))


def _fourier_kernel(x_ref, w1_ref, l2_ref, m3_ref, o_ref, a_scr, b_scr, *, rows, cols, scale3):
    npos = rows * cols
    half = o_ref.shape[1]
    slab = l2_ref.shape[1]
    rc = 1024 if npos % 1024 == 0 else npos

    def s1(r, carry):
        rs = pl.ds(pl.multiple_of(r * rc, rc), rc)
        a_scr[rs, :] = jnp.dot(x_ref[rs, :], w1_ref[0],
                               preferred_element_type=F32).astype(BF16)
        return carry

    lax.fori_loop(0, npos // rc, s1, 0)

    def s2(s, carry):
        rs = pl.ds(pl.multiple_of(s * slab, slab), slab)
        g = jnp.dot(l2_ref[...], a_scr[rs, :], preferred_element_type=F32)
        b_scr[0, rs, :] = g[0:slab, 0:half] + g[slab:, half:]
        b_scr[1, rs, :] = g[0:slab, half:] - g[slab:, 0:half]
        return carry

    lax.fori_loop(0, npos // slab, s2, 0)

    for c in range(cols):
        bc = jnp.concatenate([b_scr[0, pl.ds(c, rows, stride=cols), :],
                              b_scr[1, pl.ds(c, rows, stride=cols), :]], axis=1).astype(BF16)
        h = jnp.dot(m3_ref[...], bc, preferred_element_type=F32)
        o_ref[pl.ds(c, rows, stride=cols), :] = (h[0:rows, 0:half] + h[rows:, half:]) * scale3


def _fourier_mix(u_all, col0, batch, rows, cols, gd):
    npos = rows * cols
    half = gd // 2
    w1, l2, m3 = _fourier_tables(rows, cols, gd)
    kern = functools.partial(_fourier_kernel, rows=rows, cols=cols, scale3=1.0 / math.sqrt(rows))
    cb0 = col0 // gd
    return pl.pallas_call(
        kern,
        grid=(batch, F_GROUPS, 2),
        in_specs=[pl.BlockSpec((npos, gd), lambda b, g, h: (b, cb0 + g)),
                  pl.BlockSpec((1, gd, gd), lambda b, g, h: (h, 0, 0)),
                  pl.BlockSpec(l2.shape, lambda b, g, h: (0, 0)),
                  pl.BlockSpec(m3.shape, lambda b, g, h: (0, 0))],
        out_specs=pl.BlockSpec((npos, half), lambda b, g, h: (b, g * 2 + h)),
        out_shape=jax.ShapeDtypeStruct((batch * npos, F_GROUPS * gd), F32),
        scratch_shapes=[pltpu.VMEM((npos, gd), BF16), pltpu.VMEM((2, npos, half), F32)],
        compiler_params=_params(("arbitrary", "arbitrary", "arbitrary")),
        name="fourier_mix",
    )(u_all, w1, l2, m3)


def _octet_perm():
    n = OCTET * LANES
    p = np.zeros((n, n), np.float32)
    for k in range(OCTET):
        for g in range(OCTET):
            for h in range(S_GROUP_DIM):
                p[k * LANES + g * S_GROUP_DIM + h, g * LANES + k * S_GROUP_DIM + h] = 1.0
    return jnp.asarray(p, BF16)


def _relayout_in_kernel(x_ref, p_ref, o_ref, *, nchunk):
    io = pl.program_id(1)
    pieces = []
    for k in range(OCTET):
        pieces.append(x_ref[pl.ds(io * OCTET + k, nchunk, stride=S5_CHUNK), :].astype(BF16))
    a = jnp.concatenate(pieces, axis=1)
    b = jnp.dot(a, p_ref[...], preferred_element_type=F32).astype(BF16)
    for g in range(OCTET):
        o_ref[g] = b[:, g * LANES:(g + 1) * LANES]


def _relayout_in(us, perm):
    t, sw = us.shape
    nchunk = t // S5_CHUNK
    ngroups = sw // S_GROUP_DIM
    kern = functools.partial(_relayout_in_kernel, nchunk=nchunk)
    return pl.pallas_call(
        kern,
        grid=(sw // LANES, S5_CHUNK // OCTET),
        in_specs=[pl.BlockSpec((t, LANES), lambda o, io: (0, o)),
                  pl.BlockSpec(perm.shape, lambda o, io: (0, 0))],
        out_specs=pl.BlockSpec((OCTET, nchunk, LANES), lambda o, io: (o, 0, io)),
        out_shape=jax.ShapeDtypeStruct((ngroups, nchunk, S5_CHUNK * S_GROUP_DIM), BF16),
        compiler_params=_params(("arbitrary", "arbitrary")),
        name="s5_relayout_in",
    )(us, perm)


def _relayout_out_kernel(y_ref, p_ref, o_ref, *, nchunk):
    io = pl.program_id(1)
    b = jnp.concatenate([y_ref[g] for g in range(OCTET)], axis=1)
    a = jnp.dot(b, p_ref[...], preferred_element_type=F32)
    for k in range(OCTET):
        o_ref[pl.ds(io * OCTET + k, nchunk, stride=S5_CHUNK), :] = a[:, k * LANES:(k + 1) * LANES]


def _relayout_out(y, perm):
    ngroups, nchunk, _ = y.shape
    t = nchunk * S5_CHUNK
    sw = ngroups * S_GROUP_DIM
    kern = functools.partial(_relayout_out_kernel, nchunk=nchunk)
    return pl.pallas_call(
        kern,
        grid=(sw // LANES, S5_CHUNK // OCTET),
        in_specs=[pl.BlockSpec((OCTET, nchunk, LANES), lambda o, io: (o, 0, io)),
                  pl.BlockSpec(perm.shape, lambda o, io: (0, 0))],
        out_specs=pl.BlockSpec((t, LANES), lambda o, io: (0, o)),
        out_shape=jax.ShapeDtypeStruct((t, sw), F32),
        compiler_params=_params(("arbitrary", "arbitrary")),
        name="s5_relayout_out",
    )(y, perm)


def _s5_tables(lam_re, lam_im, log_dt, b_re, b_im, c_re, c_im, d_skip):
    lc = S5_CHUNK

    def direction(d):
        lam = lax.complex(lam_re[d].astype(F32), lam_im[d].astype(F32))
        dt = jnp.exp(log_dt[d].astype(F32))[:, None]
        z = lam * dt
        lam_bar = jnp.exp(z)
        b_bar = ((lam_bar - 1.0) / lam)[:, :, None] * lax.complex(b_re[d].astype(F32),
                                                                  b_im[d].astype(F32))
        cc = lax.complex(c_re[d].astype(F32), c_im[d].astype(F32))
        return z, b_bar, cc

    zf, bf, cf = direction(0)
    zb, bb, cb = direction(1)

    e_all = jnp.arange(-(lc - 1), lc + 1, dtype=F32)
    tf_pow = jnp.exp(zf[:, None, :] * e_all[None, :, None])
    tb_pow = jnp.exp(zb[:, None, :] * e_all[None, :, None])
    up = lambda tbl, lo: tbl[:, lo + lc - 1:lo + 2 * lc - 1]
    down = lambda tbl, hi: jnp.flip(tbl[:, hi:hi + lc], axis=1)

    def rows_of(bm):
        g, l, p, h = bm.shape
        return jnp.transpose(bm, (0, 1, 3, 2)).reshape(g, l * h, p)

    def cols_of(cm):
        g, l, h, p = cm.shape
        return jnp.transpose(cm, (0, 3, 1, 2)).reshape(g, p, l * h)

    bmf = rows_of(down(tf_pow, 0)[:, :, :, None] * bf[:, None])
    cmf = cols_of(up(tf_pow, 0)[:, :, None, :] * cf[:, None])
    bmb = rows_of(up(tb_pow, 0)[:, :, :, None] * bb[:, None])
    cmb = cols_of(down(tb_pow, 0)[:, :, None, :] * cb[:, None])
    uf = jnp.concatenate([bmf.real, -bmf.imag], axis=2)
    vf = jnp.concatenate([cmf.real, cmf.imag], axis=1)
    ub = jnp.concatenate([bmb.real, -bmb.imag], axis=2)
    vb = jnp.concatenate([cmb.real, cmb.imag], axis=1)
    pf = rows_of(down(tf_pow, lc - 1)[:, :, :, None] * bf[:, None])
    pb = bmb
    pmat = jnp.concatenate([pf.real, pb.real, pf.imag, pb.imag], axis=2)
    qf = cols_of(up(tf_pow, 1)[:, :, None, :] * cf[:, None])
    qb = cols_of(down(tb_pow, lc)[:, :, None, :] * cb[:, None])
    qmat = jnp.concatenate([qf.real, qb.real, -qf.imag, -qb.imag], axis=1)
    af = tf_pow[:, 2 * lc - 1]
    ab = tb_pow[:, 2 * lc - 1]
    decay = jnp.stack([jnp.concatenate([af.real, ab.real], axis=1),
                       jnp.concatenate([af.imag, ab.imag], axis=1)], axis=1)
    dvec = jnp.tile(d_skip.astype(F32), (1, lc))[:, None, :]
    bf16 = lambda a: a.astype(BF16)
    return bf16(uf), bf16(vf), bf16(ub), bf16(vb), bf16(pmat), bf16(qmat), decay, dvec


def _s5_kernel(z_ref, zc_ref, uf_ref, vf_ref, ub_ref, vb_ref, p_ref, q_ref, dec_ref, dv_ref,
               y_ref, e_scr, ec_scr, srf_scr, srb_scr, sif_scr, sib_scr, *, batch, nlat, nctx):
    p2 = 2 * S_STATE
    z = z_ref[0]
    e = jnp.dot(z, p_ref[0], preferred_element_type=F32)
    ec = jnp.dot(zc_ref[0], p_ref[0], preferred_element_type=F32)
    for col in range(2):
        e_scr[col] = e[:, col * p2:(col + 1) * p2]
        ec_scr[col] = ec[:, col * p2:(col + 1) * p2]

    ar = dec_ref[0, 0:1, :]
    ai = dec_ref[0, 1:2, :]
    lane = lax.broadcasted_iota(jnp.int32, (batch, p2), 1)
    is_f = lane < S_STATE

    def advance(sr, si, er, ei):
        return ar * sr - ai * si + er, ar * si + ai * sr + ei

    def pick(ref, nf, nb, n, col):
        f = ref[col, pl.ds(nf, batch, stride=n), :]
        b = ref[col, pl.ds(nb, batch, stride=n), :]
        return jnp.where(is_f, f, b)

    sr = jnp.zeros((batch, p2), F32)
    si = jnp.zeros((batch, p2), F32)
    for k in range(nctx):
        nf, nb = k, nctx - 1 - k
        sr, si = advance(sr, si, pick(ec_scr, nf, nb, nctx, 0), pick(ec_scr, nf, nb, nctx, 1))
    for k in range(nlat):
        nf, nb = k, nlat - 1 - k
        rf = pl.ds(nf, batch, stride=nlat)
        rb = pl.ds(nb, batch, stride=nlat)
        srf_scr[rf, :] = sr
        srb_scr[rb, :] = sr
        sif_scr[rf, :] = si
        sib_scr[rb, :] = si
        sr, si = advance(sr, si, pick(e_scr, nf, nb, nlat, 0), pick(e_scr, nf, nb, nlat, 1))
    lane_all = lax.broadcasted_iota(jnp.int32, srf_scr.shape, 1)
    s_re = jnp.where(lane_all < S_STATE, srf_scr[...], srb_scr[...])
    s_im = jnp.where(lane_all < S_STATE, sif_scr[...], sib_scr[...])
    s_all = jnp.concatenate([s_re, s_im], axis=1).astype(BF16)

    n = z.shape[1]
    tf = jnp.dot(uf_ref[0], vf_ref[0], preferred_element_type=F32)
    tb = jnp.dot(ub_ref[0], vb_ref[0], preferred_element_type=F32)
    ri = lax.broadcasted_iota(jnp.int32, (n, n), 0)
    ci = lax.broadcasted_iota(jnp.int32, (n, n), 1)
    rpos = ri // S_GROUP_DIM
    cpos = ci // S_GROUP_DIM
    tmat = (jnp.where(rpos <= cpos, tf, 0.0) + jnp.where(rpos >= cpos, tb, 0.0)
            + jnp.where(ri == ci, dv_ref[0], 0.0))
    y = jnp.dot(z, tmat.astype(BF16), preferred_element_type=F32)
    y = y + jnp.dot(s_all, q_ref[0], preferred_element_type=F32)
    y_ref[0] = y.astype(y_ref.dtype)


def _s5_core(zl, zc, tables, batch):
    uf, vf, ub, vb, pmat, qmat, decay, dvec = tables
    g, rl, n = zl.shape
    rcx = zc.shape[1]
    nlat, nctx = rl // batch, rcx // batch
    kern = functools.partial(_s5_kernel, batch=batch, nlat=nlat, nctx=nctx)
    gmap = lambda i: (i, 0, 0)
    spec = lambda a: pl.BlockSpec((1,) + a.shape[1:], gmap)
    return pl.pallas_call(
        kern,
        grid=(g,),
        in_specs=[spec(zl), spec(zc), spec(uf), spec(vf), spec(ub), spec(vb), spec(pmat),
                  spec(qmat), spec(decay), spec(dvec)],
        out_specs=pl.BlockSpec((1, rl, n), gmap),
        out_shape=jax.ShapeDtypeStruct((g, rl, n), BF16),
        scratch_shapes=[pltpu.VMEM((2, rl, 2 * S_STATE), F32),
                        pltpu.VMEM((2, rcx, 2 * S_STATE), F32)]
                       + [pltpu.VMEM((rl, 2 * S_STATE), F32)] * 4,
        compiler_params=_params(("arbitrary",)),
        name="s5_core",
    )(zl, zc, uf, vf, ub, vb, pmat, qmat, decay, dvec)


def _merge_kernel(fo_ref, ys_ref, zf_ref, zs_ref, x_ref, g1_ref, sh2_ref, sc2_ref,
                  wglu_ref, bglu_ref, wf_ref, ws_ref, wo_ref, l1g_ref, l1b_ref, wr_ref, br_ref,
                  xm_ref, h2_ref, lg_ref, mix_scr, h2_scr, *, alpha, rc):
    i = pl.program_id(0)
    cur = i % 2
    prv = (i + 1) % 2

    @pl.when(i == 0)
    def _():
        mix_scr[1] = jnp.zeros(mix_scr.shape[1:], F32)

    y = jax.nn.gelu(ys_ref[...], approximate=True)
    glu = jnp.dot(y.astype(BF16), wglu_ref[...], preferred_element_type=F32) + bglu_ref[...]
    s_out = y * jax.nn.sigmoid(glu)
    a = jnp.dot(fo_ref[...].astype(BF16), wf_ref[...], preferred_element_type=F32)
    b = jnp.dot(s_out.astype(BF16), ws_ref[...], preferred_element_type=F32)
    m = zf_ref[...].astype(F32) * a + zs_ref[...].astype(F32) * b
    mix_scr[cur] = jnp.dot(m.astype(BF16), wo_ref[...], preferred_element_type=F32)

    g1 = g1_ref[0]
    sh2 = sh2_ref[0]
    sc2 = 1.0 + sc2_ref[0]
    l1g = l1g_ref[...]
    l1b = l1b_ref[...]
    for r in range(x_ref.shape[0] // rc):
        rows = pl.ds(r * rc, rc)
        xm = _ln_rows(alpha * x_ref[rows, :] + g1 * mix_scr[prv, rows, :]) * l1g + l1b
        xm_ref[rows, :] = xm
        h2 = _ln_rows(xm) * sc2 + sh2
        h2_ref[rows, :] = h2
        h2_scr[rows, :] = h2.astype(BF16)
    lg_ref[...] = jnp.dot(h2_scr[...], wr_ref[...], preferred_element_type=F32) + br_ref[...]


def _merge(f_out, y_s, main, x2, gate1, shift2, scale2, wglu, bglu, wf, ws, wo, l1g, l1b, wr, br,
           seq, alpha, tm=256):
    t, d = x2.shape
    nt = t // tm
    fw, sw = f_out.shape[1], y_s.shape[1]
    kern = functools.partial(_merge_kernel, alpha=alpha, rc=32)
    cur = lambda i: jnp.minimum(i, nt - 1)
    prv = lambda i: jnp.maximum(i - 1, 0)
    rowc = lambda w: pl.BlockSpec((tm, w), lambda i: (cur(i), 0))
    rowp = lambda w: pl.BlockSpec((tm, w), lambda i: (prv(i), 0))
    mod = pl.BlockSpec((1, 1, d), lambda i: ((prv(i) * tm) // seq, 0, 0))
    full = lambda a: pl.BlockSpec(a.shape, lambda i: (0,) * a.ndim, pipeline_mode=pl.Buffered(1))
    return pl.pallas_call(
        kern,
        grid=(nt + 1,),
        in_specs=[rowc(fw), rowc(sw),
                  pl.BlockSpec((tm, d), lambda i: (cur(i), 0)),
                  pl.BlockSpec((tm, d), lambda i: (cur(i), 1)),
                  rowp(d), mod, mod, mod,
                  full(wglu), full(bglu), full(wf), full(ws), full(wo), full(l1g), full(l1b),
                  full(wr), full(br)],
        out_specs=[rowp(d), rowp(d), rowp(ROUTE_LANES)],
        out_shape=[jax.ShapeDtypeStruct((t, d), F32), jax.ShapeDtypeStruct((t, d), F32),
                   jax.ShapeDtypeStruct((t, ROUTE_LANES), F32)],
        scratch_shapes=[pltpu.VMEM((2, tm, d), F32), pltpu.VMEM((tm, d), BF16)],
        compiler_params=_params(("arbitrary",)),
        name="merge",
    )(f_out, y_s, main, main, x2, gate1, shift2, scale2, wglu, bglu, wf, ws, wo, l1g, l1b, wr, br)


def _route_kernel(lg_ref, tri_ref, info_ref, cnt_ref, run_scr):
    i = pl.program_id(0)

    @pl.when(i == 0)
    def _():
        run_scr[...] = jnp.zeros_like(run_scr)

    lg = lg_ref[...]
    tm = lg.shape[0]
    lane = lax.broadcasted_iota(jnp.int32, lg.shape, 1)
    neg = jnp.float32(-1e30)
    big = jnp.int32(ROUTE_LANES)
    ng, epg = N_EXPERT_GROUPS, EXPERTS_PER_GROUP

    is_g = lane < ng
    gl = jnp.where(is_g, lg, neg)
    gm = jnp.max(gl, axis=-1, keepdims=True)
    gidx = jnp.min(jnp.where(gl == gm, lane, big), axis=-1, keepdims=True)
    gsum = jnp.sum(jnp.where(is_g, jnp.exp(gl - gm), 0.0), axis=-1, keepdims=True)
    g_top = 1.0 / gsum

    lo = ng + gidx * epg
    el = jnp.where((lane >= lo) & (lane < lo + epg), lg, neg)
    m1 = jnp.max(el, axis=-1, keepdims=True)
    i1 = jnp.min(jnp.where(el == m1, lane, big), axis=-1, keepdims=True)
    el2 = jnp.where(lane == i1, neg, el)
    m2 = jnp.max(el2, axis=-1, keepdims=True)
    i2 = jnp.min(jnp.where(el2 == m2, lane, big), axis=-1, keepdims=True)
    tt = jnp.exp(m2 - m1)
    w0 = g_top / (1.0 + tt)
    w1 = g_top * tt / (1.0 + tt)
    e0 = i1 - ng
    e1 = i2 - ng

    oh = (lane == e0) | (lane == e1)
    pre = jnp.dot(tri_ref[...], jnp.where(oh, 1.0, 0.0).astype(BF16),
                  preferred_element_type=F32) + run_scr[...]
    r0 = jnp.sum(jnp.where(lane == e0, pre, 0.0), axis=-1, keepdims=True)
    r1 = jnp.sum(jnp.where(lane == e1, pre, 0.0), axis=-1, keepdims=True)
    run_scr[...] = run_scr[...] + jnp.sum(jnp.where(oh, 1.0, 0.0), axis=0, keepdims=True)
    cnt_ref[...] = run_scr[...]

    vals = (e0.astype(F32), e1.astype(F32), w0, w1, r0, r1)
    info = jnp.zeros(lg.shape, F32)
    for k, v in enumerate(vals):
        info = jnp.where(lane == k, v, info)
    info_ref[...] = info


def _route(logits, tm=512):
    t = logits.shape[0]
    tri = jnp.asarray(np.tril(np.ones((tm, tm), np.float32), -1), BF16)
    return pl.pallas_call(
        _route_kernel,
        grid=(t // tm,),
        in_specs=[pl.BlockSpec((tm, ROUTE_LANES), lambda i: (i, 0)),
                  pl.BlockSpec((tm, tm), lambda i: (0, 0))],
        out_specs=[pl.BlockSpec((tm, ROUTE_LANES), lambda i: (i, 0)),
                   pl.BlockSpec((1, ROUTE_LANES), lambda i: (0, 0))],
        out_shape=[jax.ShapeDtypeStruct((t, ROUTE_LANES), F32),
                   jax.ShapeDtypeStruct((1, ROUTE_LANES), F32)],
        scratch_shapes=[pltpu.VMEM((1, ROUTE_LANES), F32)],
        compiler_params=_params(("arbitrary",)),
        name="route",
    )(logits, tri)


def _row_copy(src_hbm, idx_ref, n, dst, sem):
    for r in range(n):
        pltpu.make_async_copy(src_hbm.at[pl.ds(idx_ref[0, 0, r], 1)],
                              dst.at[pl.ds(r, 1)], sem).start()


def _row_wait(src_hbm, n, dst, sem):
    for r in range(n):
        pltpu.make_async_copy(src_hbm.at[pl.ds(0, 1)], dst.at[pl.ds(r, 1)], sem).wait()


def _expert_kernel(bexp_ref, nu_ref, idx0_ref, idx1_ref, h_hbm, w1_ref, w3_ref, w2_ref,
                   y_ref, xbuf, sem):
    j = pl.program_id(0)
    nu = nu_ref[0]
    blk = xbuf.shape[1]
    slot = j % 2
    nslot = (j + 1) % 2

    @pl.when(j == 0)
    def _():
        _row_copy(h_hbm, idx0_ref, blk, xbuf.at[0], sem.at[0])

    @pl.when(j + 1 < nu)
    def _():
        _row_copy(h_hbm, idx1_ref, blk, xbuf.at[nslot], sem.at[nslot])

    @pl.when(j < nu)
    def _():
        _row_wait(h_hbm, blk, xbuf.at[slot], sem.at[slot])
        x = xbuf[slot].astype(BF16)
        a = jnp.dot(x, w1_ref[0], preferred_element_type=F32)
        b = jnp.dot(x, w3_ref[0], preferred_element_type=F32)
        hm = (a * jax.nn.sigmoid(a)) * b
        y_ref[...] = jnp.dot(hm.astype(BF16), w2_ref[0], preferred_element_type=F32)

    @pl.when(j >= nu)
    def _():
        y_ref[...] = jnp.zeros_like(y_ref)


def _experts(h2, row_tok, block_exp, n_used, w1, w3, w2):
    t, d = h2.shape
    nb = block_exp.shape[0]
    blk = MOE_BLOCK
    f = w1.shape[2]
    idx = row_tok.reshape(nb, 1, blk)
    smem = lambda m: pl.BlockSpec((1, 1, blk), m, memory_space=pltpu.SMEM)
    grid_spec = pltpu.PrefetchScalarGridSpec(
        num_scalar_prefetch=2,
        grid=(nb,),
        in_specs=[smem(lambda j, be, nu: (j, 0, 0)),
                  smem(lambda j, be, nu: (jnp.minimum(j + 1, nb - 1), 0, 0)),
                  pl.BlockSpec(memory_space=pl.ANY),
                  pl.BlockSpec((1, d, f), lambda j, be, nu: (be[j], 0, 0)),
                  pl.BlockSpec((1, d, f), lambda j, be, nu: (be[j], 0, 0)),
                  pl.BlockSpec((1, f, d), lambda j, be, nu: (be[j], 0, 0))],
        out_specs=pl.BlockSpec((blk, d), lambda j, be, nu: (j, 0)),
        scratch_shapes=[pltpu.VMEM((2, blk, d), F32), pltpu.SemaphoreType.DMA((2,))],
    )
    return pl.pallas_call(
        _expert_kernel,
        grid_spec=grid_spec,
        out_shape=jax.ShapeDtypeStruct((nb * blk, d), F32),
        compiler_params=_params(("arbitrary",)),
        name="experts",
    )(block_exp, n_used, idx, idx, h2, w1, w3, w2)


def _combine_kernel(idx0_ref, idx1_ref, yb_hbm, xm_ref, w_ref, g2_ref, lg_ref, lb_ref,
                    o_ref, ybuf, sem, *, alpha, rc):
    i = pl.program_id(0)
    n = pl.num_programs(0)
    tm = xm_ref.shape[0]
    slot = i % 2
    nslot = (i + 1) % 2

    @pl.when(i == 0)
    def _():
        _row_copy(yb_hbm, idx0_ref, 2 * tm, ybuf.at[0], sem.at[0])

    @pl.when(i + 1 < n)
    def _():
        _row_copy(yb_hbm, idx1_ref, 2 * tm, ybuf.at[nslot], sem.at[nslot])

    _row_wait(yb_hbm, 2 * tm, ybuf.at[slot], sem.at[slot])
    g2 = g2_ref[0]
    lg = lg_ref[...]
    lb = lb_ref[...]

    def body(r, carry):
        rows = pl.ds(pl.multiple_of(r * rc, rc), rc)
        rows1 = pl.ds(pl.multiple_of(tm + r * rc, rc), rc)
        w = w_ref[rows, :]
        y2 = w[:, 0:1] * ybuf[slot, rows, :] + w[:, 1:2] * ybuf[slot, rows1, :]
        o_ref[rows, :] = _ln_rows(alpha * xm_ref[rows, :] + g2 * y2) * lg + lb
        return carry

    lax.fori_loop(0, tm // rc, body, 0)


def _combine(yb, dest, x_mid, wts, gate2, l2g, l2b, seq, alpha, tm=256):
    t, d = x_mid.shape
    nt = t // tm
    idx = jnp.transpose(dest.reshape(nt, tm, 2), (0, 2, 1)).reshape(nt, 1, 2 * tm)
    kern = functools.partial(_combine_kernel, alpha=alpha, rc=32)
    smem = lambda m: pl.BlockSpec((1, 1, 2 * tm), m, memory_space=pltpu.SMEM)
    full = lambda a: pl.BlockSpec(a.shape, lambda i: (0,) * a.ndim)
    return pl.pallas_call(
        kern,
        grid=(nt,),
        in_specs=[smem(lambda i: (i, 0, 0)),
                  smem(lambda i: (jnp.minimum(i + 1, nt - 1), 0, 0)),
                  pl.BlockSpec(memory_space=pl.ANY),
                  pl.BlockSpec((tm, d), lambda i: (i, 0)),
                  pl.BlockSpec((tm, 2), lambda i: (i, 0)),
                  pl.BlockSpec((1, 1, d), lambda i: ((i * tm) // seq, 0, 0)),
                  full(l2g), full(l2b)],
        out_specs=pl.BlockSpec((tm, d), lambda i: (i, 0)),
        out_shape=jax.ShapeDtypeStruct((t, d), F32),
        scratch_shapes=[pltpu.VMEM((2, 2 * tm, d), F32), pltpu.SemaphoreType.DMA((2,))],
        compiler_params=_params(("arbitrary",)),
        name="combine",
    )(idx, idx, yb, x_mid, wts, gate2, l2g, l2b)


def kernel(x, c, ctx, c_ctx, w_ada, b_ada, w_in, w_f_proj, lam_re, lam_im, log_dt, b_re, b_im,
           c_re, c_im, d_skip, w_glu, b_glu, w_s_proj, w_out, ln1_g, ln1_b, w_group, b_group,
           w_expert, b_expert, w1, w3, w2, ln2_g, ln2_b):
    depth = w_ada.shape[0]
    assert depth == 1, "single-layer block"
    bsz, seq, d = x.shape
    ctx_len = ctx.shape[1]
    t = bsz * seq
    fw = w_f_proj.shape[1]
    sw = w_s_proj.shape[1]
    gd = fw // F_GROUPS
    rows = seq // GRID_W
    alpha = (2.0 * depth) ** 0.25
    row2 = lambda v: v.reshape(1, -1).astype(F32)

    cc = jnp.concatenate([c, c_ctx[None], jnp.zeros((8 - bsz - 1, d), F32)], axis=0)
    mod = _adaln(cc, w_ada[0], b_ada[0].reshape(1, -1))
    mods = [mod[:, k * d:(k + 1) * d] for k in range(N_MOD)]
    per_b = lambda m: m[:bsz].reshape(bsz, 1, d)
    ctx_row = lambda m: m[bsz:bsz + 1].reshape(1, 1, d)

    wi = w_in[0]
    wi = jnp.concatenate([wi[:, fw + sw:], wi[:, :fw + sw]], axis=1).astype(BF16)
    tn = sw
    n_sig = (2 * d) // tn
    main, us = _ln_mod_matmul(x.reshape(t, d), per_b(mods[0]), per_b(mods[1]), wi,
                              rows_per_mod=seq, n_sig=n_sig, tm=1024, tn=tn)
    w_us = w_in[0][:, fw:fw + sw].astype(BF16)
    _, us_ctx = _ln_mod_matmul(ctx.reshape(bsz * ctx_len, d), ctx_row(mods[0]), ctx_row(mods[1]),
                               w_us, rows_per_mod=bsz * ctx_len, n_sig=0, tm=ctx_len, tn=tn)

    f_out = _fourier_mix(main, 2 * d, bsz, rows, GRID_W, gd)

    perm = _octet_perm()
    tables = _s5_tables(lam_re[0], lam_im[0], log_dt[0], b_re[0], b_im[0], c_re[0], c_im[0],
                        d_skip[0])
    zl = _relayout_in(us, perm)
    zc = _relayout_in(us_ctx, perm)
    y_chunks = _s5_core(zl, zc, tables, bsz)
    y_s = _relayout_out(y_chunks, perm)

    wr = jnp.concatenate([w_group[0], w_expert[0],
                          jnp.zeros((d, ROUTE_LANES - N_EXPERT_GROUPS - N_EXPERTS), F32)], axis=1)
    br = jnp.concatenate([b_group[0], b_expert[0],
                          jnp.zeros((ROUTE_LANES - N_EXPERT_GROUPS - N_EXPERTS,), F32)])
    x_mid, h2, logits = _merge(
        f_out, y_s, main, x.reshape(t, d), per_b(mods[2]), per_b(mods[3]), per_b(mods[4]),
        w_glu[0].astype(BF16), row2(b_glu[0]), w_f_proj[0].astype(BF16), w_s_proj[0].astype(BF16),
        w_out[0].astype(BF16), row2(ln1_g[0]), row2(ln1_b[0]), wr.astype(BF16), row2(br),
        seq, alpha)

    info, cnt = _route(logits)
    experts = info[:, 0:2].astype(jnp.int32)
    wts = info[:, 2:4]
    rank = info[:, 4:6].astype(jnp.int32)
    counts = cnt[0, :N_EXPERTS].astype(jnp.int32)
    blk = MOE_BLOCK
    n_assign = 2 * t
    nb = -(-(n_assign + N_EXPERTS * (blk - 1)) // blk)
    padded = (counts + blk - 1) // blk * blk
    pend = jnp.cumsum(padded)
    pstart = pend - padded
    dest = pstart[experts] + rank
    n_used = (pend[-1] // blk).astype(jnp.int32).reshape(1)
    block_row = jnp.arange(nb, dtype=jnp.int32) * blk
    block_exp = jnp.minimum(jnp.sum((pend[None, :] <= block_row[:, None]).astype(jnp.int32), axis=1),
                            N_EXPERTS - 1)
    tok = jnp.repeat(jnp.arange(t, dtype=jnp.int32), 2)
    row_tok = jnp.zeros((nb * blk,), jnp.int32).at[dest.reshape(-1)].set(
        tok, unique_indices=True, mode="promise_in_bounds")

    yb = _experts(h2, row_tok, block_exp, n_used, w1[0].astype(BF16), w3[0].astype(BF16),
                  w2[0].astype(BF16))
    out = _combine(yb, dest, x_mid, wts, per_b(mods[5]), row2(ln2_g[0]), row2(ln2_b[0]), seq, alpha)
    return out.reshape(bsz, seq, d)
```

```python
import functools
import math

import numpy as np
import jax
import jax.numpy as jnp
from jax import lax
from jax.experimental import pallas as pl
from jax.experimental.pallas import tpu as pltpu

F32 = jnp.float32
BF16 = jnp.bfloat16

GRID_W = 64
F_GROUPS = 4
S_GROUP_DIM = 16
S_STATE = 64
N_EXPERT_GROUPS = 4
EXPERTS_PER_GROUP = 8
N_EXPERTS = N_EXPERT_GROUPS * EXPERTS_PER_GROUP
N_MOD = 6
LN_EPS = 1e-5

LANES = 128
S5_CHUNK = 64
OCTET = LANES // S_GROUP_DIM
MOE_BLOCK = 256
ROUTE_LANES = 128
VMEM_LIMIT = 56 * 1024 * 1024


def _params(sem, vmem=VMEM_LIMIT):
    return pltpu.CompilerParams(dimension_semantics=sem, vmem_limit_bytes=vmem)


def _ln_rows(x):
    mu = jnp.mean(x, axis=-1, keepdims=True)
    xc = x - mu
    var = jnp.mean(xc * xc, axis=-1, keepdims=True)
    return xc * lax.rsqrt(var + LN_EPS)


def _adaln_kernel(c_ref, w_ref, b_ref, o_ref):
    s = c_ref[...]
    s = s * jax.nn.sigmoid(s)
    o_ref[...] = jnp.dot(s.astype(BF16), w_ref[...].astype(BF16),
                         preferred_element_type=F32) + b_ref[...]


def _adaln(cc, w, b, tn=1024):
    m, d = cc.shape
    n = w.shape[1]
    return pl.pallas_call(
        _adaln_kernel,
        grid=(n // tn,),
        in_specs=[pl.BlockSpec((m, d), lambda j: (0, 0)),
                  pl.BlockSpec((d, tn), lambda j: (0, j)),
                  pl.BlockSpec((1, tn), lambda j: (0, j))],
        out_specs=pl.BlockSpec((m, tn), lambda j: (0, j)),
        out_shape=jax.ShapeDtypeStruct((m, n), F32),
        compiler_params=_params(("arbitrary",)),
        name="adaln",
    )(cc, w, b)


def _ln_mod_rows(x_ref, rows, sc1, sh):
    mu = jnp.mean(x_ref[rows, :], axis=-1, keepdims=True)
    xc = x_ref[rows, :] - mu
    var = jnp.mean(xc * xc, axis=-1, keepdims=True)
    return (x_ref[rows, :] - mu) * lax.rsqrt(var + LN_EPS) * sc1 + sh


def _lnmm_kernel(x0_ref, sh0_ref, sc0_ref, xn_ref, shn_ref, scn_ref, w_ref, o_ref, us_ref,
                 ha, hb, *, n_sig, nrc, mchunks):
    i = pl.program_id(0)
    j = pl.program_id(1)
    tm = xn_ref.shape[0]
    rcs = tm // nrc

    @pl.when((i == 0) & (j == 0))
    def _():
        def body(r, carry):
            rows = pl.ds(pl.multiple_of(r * rcs, rcs), rcs)
            ha[rows, :] = _ln_mod_rows(x0_ref, rows, 1.0 + sc0_ref[0], sh0_ref[0]).astype(BF16)
            return carry

        lax.fori_loop(0, nrc, body, 0)

    def step(h_cur, h_nxt):
        tmc = tm // mchunks
        for m in range(mchunks):
            rs = pl.ds(m * tmc, tmc)
            acc = jnp.dot(h_cur[rs, :], w_ref[...], preferred_element_type=F32)
            o_ref[rs, :] = jnp.where(j < n_sig, jax.nn.sigmoid(acc), acc).astype(o_ref.dtype)
            us_ref[rs, :] = acc
        c = jnp.minimum(j, nrc - 1)
        rows = pl.ds(pl.multiple_of(c * rcs, rcs), rcs)
        h_nxt[rows, :] = _ln_mod_rows(xn_ref, rows, 1.0 + scn_ref[0], shn_ref[0]).astype(BF16)

    @pl.when(i % 2 == 0)
    def _():
        step(ha, hb)

    @pl.when(i % 2 == 1)
    def _():
        step(hb, ha)


def _ln_mod_matmul(x2, shift, scale, w, rows_per_mod, n_sig, tm, tn):
    t, d = x2.shape
    n = w.shape[1]
    nt, nj = t // tm, n // tn
    nrc = min(8, nj)
    kern = functools.partial(_lnmm_kernel, n_sig=n_sig, nrc=nrc, mchunks=max(1, tm // 256))
    nxt = lambda i: jnp.minimum(i + 1, nt - 1)
    once = pl.Buffered(1)
    mod0 = pl.BlockSpec((1, 1, d), lambda i, j: (0, 0, 0), pipeline_mode=once)
    modn = pl.BlockSpec((1, 1, d), lambda i, j: ((nxt(i) * tm) // rows_per_mod, 0, 0))
    return pl.pallas_call(
        kern,
        grid=(nt, nj),
        in_specs=[pl.BlockSpec((tm, d), lambda i, j: (0, 0), pipeline_mode=once), mod0, mod0,
                  pl.BlockSpec((tm, d), lambda i, j: (nxt(i), 0)), modn, modn,
                  pl.BlockSpec((d, tn), lambda i, j: (0, j))],
        out_specs=[pl.BlockSpec((tm, tn), lambda i, j: (i, j)),
                   pl.BlockSpec((tm, tn), lambda i, j: (i, 0))],
        out_shape=[jax.ShapeDtypeStruct((t, n), BF16),
                   jax.ShapeDtypeStruct((t, tn), F32)],
        scratch_shapes=[pltpu.VMEM((tm, d), BF16), pltpu.VMEM((tm, d), BF16)],
        compiler_params=_params(("arbitrary", "arbitrary")),
        name="ln_mod_matmul",
    )(x2, shift, scale, x2, shift, scale, w)


def _dft_cs(n):
    k = np.arange(n)
    ang = 2.0 * np.pi * np.outer(k, k) / n
    return np.cos(ang), np.sin(ang)


def _fourier_tables(rows, cols, gd):
    c1, s1 = _dft_cs(gd)
    half = gd // 2
    w1 = np.stack([np.concatenate([c1[:, h * half:(h + 1) * half],
                                   -s1[:, h * half:(h + 1) * half]], axis=1)
                   for h in range(2)]) / math.sqrt(gd)
    c2, s2 = _dft_cs(cols)
    rep = 256 // cols
    eye = np.eye(rep)
    l2 = np.concatenate([np.kron(eye, c2), np.kron(eye, s2)], axis=0) / math.sqrt(cols)
    c3, s3 = _dft_cs(rows)
    m3 = np.concatenate([c3, s3], axis=0)
    return (jnp.asarray(w1, BF16), jnp.asarray(l2, BF16), jnp.asarray(m3, BF16))


def _fourier_kernel(x_ref, w1_ref, l2_ref, m3_ref, o_ref, a_scr, b_scr, *, rows, cols, scale3):
    npos = rows * cols
    half = o_ref.shape[1]
    slab = l2_ref.shape[1]
    rc = 1024 if npos % 1024 == 0 else npos

    def s1(r, carry):
        rs = pl.ds(pl.multiple_of(r * rc, rc), rc)
        a_scr[rs, :] = jnp.dot(x_ref[rs, :], w1_ref[0],
                               preferred_element_type=F32).astype(BF16)
        return carry

    lax.fori_loop(0, npos // rc, s1, 0)

    unroll = 4 if (npos // slab) % 4 == 0 else 1

    def s2(s, carry):
        for k in range(unroll):
            rs = pl.ds(pl.multiple_of((s * unroll + k) * slab, slab), slab)
            g = jnp.dot(l2_ref[...], a_scr[rs, :], preferred_element_type=F32)
            b_scr[0, rs, :] = g[0:slab, 0:half] + g[slab:, half:]
            b_scr[1, rs, :] = g[0:slab, half:] - g[slab:, 0:half]
        return carry

    lax.fori_loop(0, npos // slab // unroll, s2, 0)

    for c in range(cols):
        bc = jnp.concatenate([b_scr[0, pl.ds(c, rows, stride=cols), :],
                              b_scr[1, pl.ds(c, rows, stride=cols), :]], axis=1).astype(BF16)
        h = jnp.dot(m3_ref[...], bc, preferred_element_type=F32)
        o_ref[pl.ds(c, rows, stride=cols), :] = (h[0:rows, 0:half] + h[rows:, half:]) * scale3


def _fourier_mix(u_all, col0, batch, rows, cols, gd):
    npos = rows * cols
    half = gd // 2
    w1, l2, m3 = _fourier_tables(rows, cols, gd)
    kern = functools.partial(_fourier_kernel, rows=rows, cols=cols, scale3=1.0 / math.sqrt(rows))
    cb0 = col0 // gd
    return pl.pallas_call(
        kern,
        grid=(batch, F_GROUPS, 2),
        in_specs=[pl.BlockSpec((npos, gd), lambda b, g, h: (b, cb0 + g)),
                  pl.BlockSpec((1, gd, gd), lambda b, g, h: (h, 0, 0)),
                  pl.BlockSpec(l2.shape, lambda b, g, h: (0, 0)),
                  pl.BlockSpec(m3.shape, lambda b, g, h: (0, 0))],
        out_specs=pl.BlockSpec((npos, half), lambda b, g, h: (b, g * 2 + h)),
        out_shape=jax.ShapeDtypeStruct((batch * npos, F_GROUPS * gd), F32),
        scratch_shapes=[pltpu.VMEM((npos, gd), BF16), pltpu.VMEM((2, npos, half), F32)],
        compiler_params=_params(("arbitrary", "arbitrary", "arbitrary")),
        name="fourier_mix",
    )(u_all, w1, l2, m3)


def _octet_perm():
    n = OCTET * LANES
    p = np.zeros((n, n), np.float32)
    for k in range(OCTET):
        for g in range(OCTET):
            for h in range(S_GROUP_DIM):
                p[k * LANES + g * S_GROUP_DIM + h, g * LANES + k * S_GROUP_DIM + h] = 1.0
    return jnp.asarray(p, BF16)


def _relayout_in_kernel(x_ref, p_ref, o_ref, *, nchunk):
    io = pl.program_id(1)
    pieces = []
    for k in range(OCTET):
        pieces.append(x_ref[pl.ds(io * OCTET + k, nchunk, stride=S5_CHUNK), :].astype(BF16))
    a = jnp.concatenate(pieces, axis=1)
    b = jnp.dot(a, p_ref[...], preferred_element_type=F32).astype(BF16)
    for g in range(OCTET):
        o_ref[g] = b[:, g * LANES:(g + 1) * LANES]


def _relayout_in(us, perm):
    t, sw = us.shape
    nchunk = t // S5_CHUNK
    ngroups = sw // S_GROUP_DIM
    kern = functools.partial(_relayout_in_kernel, nchunk=nchunk)
    return pl.pallas_call(
        kern,
        grid=(sw // LANES, S5_CHUNK // OCTET),
        in_specs=[pl.BlockSpec((t, LANES), lambda o, io: (0, o)),
                  pl.BlockSpec(perm.shape, lambda o, io: (0, 0))],
        out_specs=pl.BlockSpec((OCTET, nchunk, LANES), lambda o, io: (o, 0, io)),
        out_shape=jax.ShapeDtypeStruct((ngroups, nchunk, S5_CHUNK * S_GROUP_DIM), BF16),
        compiler_params=_params(("arbitrary", "arbitrary")),
        name="s5_relayout_in",
    )(us, perm)


def _relayout_out_kernel(y_ref, p_ref, o_ref, *, nchunk):
    io = pl.program_id(1)
    b = jnp.concatenate([y_ref[g] for g in range(OCTET)], axis=1)
    a = jnp.dot(b, p_ref[...], preferred_element_type=F32)
    for k in range(OCTET):
        o_ref[pl.ds(io * OCTET + k, nchunk, stride=S5_CHUNK), :] = a[:, k * LANES:(k + 1) * LANES]


def _relayout_out(y, perm):
    ngroups, nchunk, _ = y.shape
    t = nchunk * S5_CHUNK
    sw = ngroups * S_GROUP_DIM
    kern = functools.partial(_relayout_out_kernel, nchunk=nchunk)
    return pl.pallas_call(
        kern,
        grid=(sw // LANES, S5_CHUNK // OCTET),
        in_specs=[pl.BlockSpec((OCTET, nchunk, LANES), lambda o, io: (o, 0, io)),
                  pl.BlockSpec(perm.shape, lambda o, io: (0, 0))],
        out_specs=pl.BlockSpec((t, LANES), lambda o, io: (0, o)),
        out_shape=jax.ShapeDtypeStruct((t, sw), F32),
        compiler_params=_params(("arbitrary", "arbitrary")),
        name="s5_relayout_out",
    )(y, perm)


def _s5_tables(lam_re, lam_im, log_dt, b_re, b_im, c_re, c_im, d_skip):
    lc = S5_CHUNK

    def direction(d):
        lam = lax.complex(lam_re[d].astype(F32), lam_im[d].astype(F32))
        dt = jnp.exp(log_dt[d].astype(F32))[:, None]
        z = lam * dt
        lam_bar = jnp.exp(z)
        b_bar = ((lam_bar - 1.0) / lam)[:, :, None] * lax.complex(b_re[d].astype(F32),
                                                                  b_im[d].astype(F32))
        cc = lax.complex(c_re[d].astype(F32), c_im[d].astype(F32))
        return z, b_bar, cc

    zf, bf, cf = direction(0)
    zb, bb, cb = direction(1)

    e_all = jnp.arange(-(lc - 1), lc + 1, dtype=F32)
    tf_pow = jnp.exp(zf[:, None, :] * e_all[None, :, None])
    tb_pow = jnp.exp(zb[:, None, :] * e_all[None, :, None])
    up = lambda tbl, lo: tbl[:, lo + lc - 1:lo + 2 * lc - 1]
    down = lambda tbl, hi: jnp.flip(tbl[:, hi:hi + lc], axis=1)

    def rows_of(bm):
        g, l, p, h = bm.shape
        return jnp.transpose(bm, (0, 1, 3, 2)).reshape(g, l * h, p)

    def cols_of(cm):
        g, l, h, p = cm.shape
        return jnp.transpose(cm, (0, 3, 1, 2)).reshape(g, p, l * h)

    bmf = rows_of(down(tf_pow, 0)[:, :, :, None] * bf[:, None])
    cmf = cols_of(up(tf_pow, 0)[:, :, None, :] * cf[:, None])
    bmb = rows_of(up(tb_pow, 0)[:, :, :, None] * bb[:, None])
    cmb = cols_of(down(tb_pow, 0)[:, :, None, :] * cb[:, None])
    uf = jnp.concatenate([bmf.real, -bmf.imag], axis=2)
    vf = jnp.concatenate([cmf.real, cmf.imag], axis=1)
    ub = jnp.concatenate([bmb.real, -bmb.imag], axis=2)
    vb = jnp.concatenate([cmb.real, cmb.imag], axis=1)
    pf = rows_of(down(tf_pow, lc - 1)[:, :, :, None] * bf[:, None])
    pb = bmb
    pmat = jnp.concatenate([pf.real, pb.real, pf.imag, pb.imag], axis=2)
    qf = cols_of(up(tf_pow, 1)[:, :, None, :] * cf[:, None])
    qb = cols_of(down(tb_pow, lc)[:, :, None, :] * cb[:, None])
    qmat = jnp.concatenate([qf.real, qb.real, -qf.imag, -qb.imag], axis=1)
    af = tf_pow[:, 2 * lc - 1]
    ab = tb_pow[:, 2 * lc - 1]
    decay = jnp.stack([jnp.concatenate([af.real, ab.real], axis=1),
                       jnp.concatenate([af.imag, ab.imag], axis=1)], axis=1)
    dvec = jnp.tile(d_skip.astype(F32), (1, lc))[:, None, :]
    bf16 = lambda a: a.astype(BF16)
    return bf16(uf), bf16(vf), bf16(ub), bf16(vb), bf16(pmat), bf16(qmat), decay, dvec


def _s5_kernel(z_ref, zc_ref, uf_ref, vf_ref, ub_ref, vb_ref, p_ref, q_ref, dec_ref, dv_ref,
               y_ref, e_scr, ec_scr, srf_scr, srb_scr, sif_scr, sib_scr, *, batch, nlat, nctx):
    p2 = 2 * S_STATE
    z = z_ref[0]
    e = jnp.dot(z, p_ref[0], preferred_element_type=F32)
    ec = jnp.dot(zc_ref[0], p_ref[0], preferred_element_type=F32)
    for col in range(2):
        e_scr[col] = e[:, col * p2:(col + 1) * p2]
        ec_scr[col] = ec[:, col * p2:(col + 1) * p2]

    ar = dec_ref[0, 0:1, :]
    ai = dec_ref[0, 1:2, :]
    lane = lax.broadcasted_iota(jnp.int32, (batch, p2), 1)
    is_f = lane < S_STATE

    def advance(sr, si, er, ei):
        return ar * sr - ai * si + er, ar * si + ai * sr + ei

    def pick(ref, nf, nb, n, col):
        f = ref[col, pl.ds(nf, batch, stride=n), :]
        b = ref[col, pl.ds(nb, batch, stride=n), :]
        return jnp.where(is_f, f, b)

    sr = jnp.zeros((batch, p2), F32)
    si = jnp.zeros((batch, p2), F32)
    for k in range(nctx):
        nf, nb = k, nctx - 1 - k
        sr, si = advance(sr, si, pick(ec_scr, nf, nb, nctx, 0), pick(ec_scr, nf, nb, nctx, 1))
    for k in range(nlat):
        nf, nb = k, nlat - 1 - k
        rf = pl.ds(nf, batch, stride=nlat)
        rb = pl.ds(nb, batch, stride=nlat)
        srf_scr[rf, :] = sr
        srb_scr[rb, :] = sr
        sif_scr[rf, :] = si
        sib_scr[rb, :] = si
        sr, si = advance(sr, si, pick(e_scr, nf, nb, nlat, 0), pick(e_scr, nf, nb, nlat, 1))
    lane_all = lax.broadcasted_iota(jnp.int32, srf_scr.shape, 1)
    s_re = jnp.where(lane_all < S_STATE, srf_scr[...], srb_scr[...])
    s_im = jnp.where(lane_all < S_STATE, sif_scr[...], sib_scr[...])
    s_all = jnp.concatenate([s_re, s_im], axis=1).astype(BF16)

    n = z.shape[1]
    tf = jnp.dot(uf_ref[0], vf_ref[0], preferred_element_type=F32)
    tb = jnp.dot(ub_ref[0], vb_ref[0], preferred_element_type=F32)
    ri = lax.broadcasted_iota(jnp.int32, (n, n), 0)
    ci = lax.broadcasted_iota(jnp.int32, (n, n), 1)
    rpos = ri // S_GROUP_DIM
    cpos = ci // S_GROUP_DIM
    tmat = (jnp.where(rpos <= cpos, tf, 0.0) + jnp.where(rpos >= cpos, tb, 0.0)
            + jnp.where(ri == ci, dv_ref[0], 0.0))
    y = jnp.dot(z, tmat.astype(BF16), preferred_element_type=F32)
    y = y + jnp.dot(s_all, q_ref[0], preferred_element_type=F32)
    y_ref[0] = y.astype(y_ref.dtype)


def _s5_core(zl, zc, tables, batch):
    uf, vf, ub, vb, pmat, qmat, decay, dvec = tables
    g, rl, n = zl.shape
    rcx = zc.shape[1]
    nlat, nctx = rl // batch, rcx // batch
    kern = functools.partial(_s5_kernel, batch=batch, nlat=nlat, nctx=nctx)
    gmap = lambda i: (i, 0, 0)
    spec = lambda a: pl.BlockSpec((1,) + a.shape[1:], gmap)
    return pl.pallas_call(
        kern,
        grid=(g,),
        in_specs=[spec(zl), spec(zc), spec(uf), spec(vf), spec(ub), spec(vb), spec(pmat),
                  spec(qmat), spec(decay), spec(dvec)],
        out_specs=pl.BlockSpec((1, rl, n), gmap),
        out_shape=jax.ShapeDtypeStruct((g, rl, n), BF16),
        scratch_shapes=[pltpu.VMEM((2, rl, 2 * S_STATE), F32),
                        pltpu.VMEM((2, rcx, 2 * S_STATE), F32)]
                       + [pltpu.VMEM((rl, 2 * S_STATE), F32)] * 4,
        compiler_params=_params(("arbitrary",)),
        name="s5_core",
    )(zl, zc, uf, vf, ub, vb, pmat, qmat, decay, dvec)


def _merge_kernel(fo_ref, ys_ref, zf_ref, zs_ref, x_ref, g1_ref, sh2_ref, sc2_ref,
                  wglu_ref, bglu_ref, wf_ref, ws_ref, wo_ref, l1g_ref, l1b_ref, wr_ref, br_ref,
                  xm_ref, h2_ref, lg_ref, mix_a, mix_b, h2_scr, *, alpha, rc):
    i = pl.program_id(0)

    @pl.when(i == 0)
    def _():
        mix_b[...] = jnp.zeros_like(mix_b)

    def step(mix_cur, mix_prv):
        y = jax.nn.gelu(ys_ref[...], approximate=True)
        glu = jnp.dot(y.astype(BF16), wglu_ref[...], preferred_element_type=F32) + bglu_ref[...]
        s_out = y * jax.nn.sigmoid(glu)
        a = jnp.dot(fo_ref[...].astype(BF16), wf_ref[...], preferred_element_type=F32)
        b = jnp.dot(s_out.astype(BF16), ws_ref[...], preferred_element_type=F32)
        m = zf_ref[...].astype(F32) * a + zs_ref[...].astype(F32) * b
        mix_cur[...] = jnp.dot(m.astype(BF16), wo_ref[...], preferred_element_type=F32)

        g1 = g1_ref[0]
        sh2 = sh2_ref[0]
        sc2 = 1.0 + sc2_ref[0]
        l1g = l1g_ref[...]
        l1b = l1b_ref[...]
        for r in range(x_ref.shape[0] // rc):
            rows = pl.ds(r * rc, rc)
            xm = _ln_rows(alpha * x_ref[rows, :] + g1 * mix_prv[rows, :]) * l1g + l1b
            xm_ref[rows, :] = xm
            h2 = _ln_rows(xm) * sc2 + sh2
            h2_ref[rows, :] = h2
            h2_scr[rows, :] = h2.astype(BF16)
        lg_ref[...] = jnp.dot(h2_scr[...], wr_ref[...], preferred_element_type=F32) + br_ref[...]

    @pl.when(i % 2 == 0)
    def _():
        step(mix_a, mix_b)

    @pl.when(i % 2 == 1)
    def _():
        step(mix_b, mix_a)


def _merge(f_out, y_s, main, x2, gate1, shift2, scale2, wglu, bglu, wf, ws, wo, l1g, l1b, wr, br,
           seq, alpha, tm=256):
    t, d = x2.shape
    nt = t // tm
    fw, sw = f_out.shape[1], y_s.shape[1]
    kern = functools.partial(_merge_kernel, alpha=alpha, rc=32)
    cur = lambda i: jnp.minimum(i, nt - 1)
    prv = lambda i: jnp.maximum(i - 1, 0)
    rowc = lambda w: pl.BlockSpec((tm, w), lambda i: (cur(i), 0))
    rowp = lambda w: pl.BlockSpec((tm, w), lambda i: (prv(i), 0))
    mod = pl.BlockSpec((1, 1, d), lambda i: ((prv(i) * tm) // seq, 0, 0))
    full = lambda a: pl.BlockSpec(a.shape, lambda i: (0,) * a.ndim, pipeline_mode=pl.Buffered(1))
    return pl.pallas_call(
        kern,
        grid=(nt + 1,),
        in_specs=[rowc(fw), rowc(sw),
                  pl.BlockSpec((tm, d), lambda i: (cur(i), 0)),
                  pl.BlockSpec((tm, d), lambda i: (cur(i), 1)),
                  rowp(d), mod, mod, mod,
                  full(wglu), full(bglu), full(wf), full(ws), full(wo), full(l1g), full(l1b),
                  full(wr), full(br)],
        out_specs=[rowp(d), rowp(d), rowp(ROUTE_LANES)],
        out_shape=[jax.ShapeDtypeStruct((t, d), F32), jax.ShapeDtypeStruct((t, d), F32),
                   jax.ShapeDtypeStruct((t, ROUTE_LANES), F32)],
        scratch_shapes=[pltpu.VMEM((tm, d), F32), pltpu.VMEM((tm, d), F32),
                        pltpu.VMEM((tm, d), BF16)],
        compiler_params=_params(("arbitrary",)),
        name="merge",
    )(f_out, y_s, main, main, x2, gate1, shift2, scale2, wglu, bglu, wf, ws, wo, l1g, l1b, wr, br)


def _route_kernel(lg_ref, tri_ref, info_ref, cnt_ref, run_scr):
    i = pl.program_id(0)

    @pl.when(i == 0)
    def _():
        run_scr[...] = jnp.zeros_like(run_scr)

    lg = lg_ref[...]
    tm = lg.shape[0]
    lane = lax.broadcasted_iota(jnp.int32, lg.shape, 1)
    neg = jnp.float32(-1e30)
    big = jnp.int32(ROUTE_LANES)
    ng, epg = N_EXPERT_GROUPS, EXPERTS_PER_GROUP

    is_g = lane < ng
    gl = jnp.where(is_g, lg, neg)
    gm = jnp.max(gl, axis=-1, keepdims=True)
    gidx = jnp.min(jnp.where(gl == gm, lane, big), axis=-1, keepdims=True)
    gsum = jnp.sum(jnp.where(is_g, jnp.exp(gl - gm), 0.0), axis=-1, keepdims=True)
    g_top = 1.0 / gsum

    lo = ng + gidx * epg
    el = jnp.where((lane >= lo) & (lane < lo + epg), lg, neg)
    m1 = jnp.max(el, axis=-1, keepdims=True)
    i1 = jnp.min(jnp.where(el == m1, lane, big), axis=-1, keepdims=True)
    el2 = jnp.where(lane == i1, neg, el)
    m2 = jnp.max(el2, axis=-1, keepdims=True)
    i2 = jnp.min(jnp.where(el2 == m2, lane, big), axis=-1, keepdims=True)
    tt = jnp.exp(m2 - m1)
    w0 = g_top / (1.0 + tt)
    w1 = g_top * tt / (1.0 + tt)
    e0 = i1 - ng
    e1 = i2 - ng

    oh = (lane == e0) | (lane == e1)
    pre = jnp.dot(tri_ref[...], jnp.where(oh, 1.0, 0.0).astype(BF16),
                  preferred_element_type=F32) + run_scr[...]
    r0 = jnp.sum(jnp.where(lane == e0, pre, 0.0), axis=-1, keepdims=True)
    r1 = jnp.sum(jnp.where(lane == e1, pre, 0.0), axis=-1, keepdims=True)
    run_scr[...] = run_scr[...] + jnp.sum(jnp.where(oh, 1.0, 0.0), axis=0, keepdims=True)
    cnt_ref[...] = run_scr[...]

    vals = (e0.astype(F32), e1.astype(F32), w0, w1, r0, r1)
    info = jnp.zeros(lg.shape, F32)
    for k, v in enumerate(vals):
        info = jnp.where(lane == k, v, info)
    info_ref[...] = info


def _route(logits, tm=512):
    t = logits.shape[0]
    tri = jnp.asarray(np.tril(np.ones((tm, tm), np.float32), -1), BF16)
    return pl.pallas_call(
        _route_kernel,
        grid=(t // tm,),
        in_specs=[pl.BlockSpec((tm, ROUTE_LANES), lambda i: (i, 0)),
                  pl.BlockSpec((tm, tm), lambda i: (0, 0))],
        out_specs=[pl.BlockSpec((tm, ROUTE_LANES), lambda i: (i, 0)),
                   pl.BlockSpec((1, ROUTE_LANES), lambda i: (0, 0))],
        out_shape=[jax.ShapeDtypeStruct((t, ROUTE_LANES), F32),
                   jax.ShapeDtypeStruct((1, ROUTE_LANES), F32)],
        scratch_shapes=[pltpu.VMEM((1, ROUTE_LANES), F32)],
        compiler_params=_params(("arbitrary",)),
        name="route",
    )(logits, tri)


def _pack_bf16_pairs(lo, hi):
    def rnd(v):
        bits = lax.bitcast_convert_type(v, jnp.uint32)
        return bits + (jnp.uint32(0x7FFF) + ((bits >> 16) & jnp.uint32(1)))
    return (rnd(hi) & jnp.uint32(0xFFFF0000)) | (rnd(lo) >> 16)


def _unpack_bf16_pairs(w):
    lo = lax.bitcast_convert_type(w << 16, F32)
    hi = lax.bitcast_convert_type(w & jnp.uint32(0xFFFF0000), F32)
    return lo, hi


def _gather_rows(src_hbm, idx_ref, dst, sem):
    for r in range(dst.shape[0]):
        pltpu.make_async_copy(src_hbm.at[pl.ds(idx_ref[0, 0, r], 1)],
                              dst.at[pl.ds(r, 1)], sem).start()


def _wait_rows(any_hbm, buf, sem):
    for r in range(buf.shape[0]):
        pltpu.make_async_copy(any_hbm.at[pl.ds(0, 1)], buf.at[pl.ds(r, 1)], sem).wait()


def _expert_kernel(bexp_ref, nu_ref, g0_ref, g1_ref, s_ref, h_hbm, w1_ref, w3_ref, w2_ref,
                   o_hbm, xa, xb, ya, yb, gsem, ssem):
    j = pl.program_id(0)
    nu = nu_ref[0]
    half = ya.shape[1]

    @pl.when(j == 0)
    def _():
        ya[...] = jnp.zeros_like(ya)
        yb[...] = jnp.zeros_like(yb)
        _gather_rows(h_hbm, g0_ref, xa, gsem.at[0])

    def body(xc, xn, yc, yp, gs_c, gs_n):
        _wait_rows(h_hbm, xc, gs_c)
        for r in range(xc.shape[0]):
            pltpu.make_async_copy(yp.at[pl.ds(r, 1)], o_hbm.at[pl.ds(s_ref[0, 0, r], 1)],
                                  ssem.at[0]).start()
            pltpu.make_async_copy(h_hbm.at[pl.ds(g1_ref[0, 0, r], 1)], xn.at[pl.ds(r, 1)],
                                  gs_n).start()
        x = xc[...].astype(BF16)
        a = jnp.dot(x, w1_ref[0], preferred_element_type=F32)
        b = jnp.dot(x, w3_ref[0], preferred_element_type=F32)
        hm = (a * jax.nn.sigmoid(a)) * b
        y = jnp.dot(hm.astype(BF16), w2_ref[0], preferred_element_type=F32)
        yc[...] = _pack_bf16_pairs(y[:, :half], y[:, half:])
        _wait_rows(o_hbm, yp, ssem.at[0])

    @pl.when((j <= nu) & (j % 2 == 0))
    def _():
        body(xa, xb, ya, yb, gsem.at[0], gsem.at[1])

    @pl.when((j <= nu) & (j % 2 == 1))
    def _():
        body(xb, xa, yb, ya, gsem.at[1], gsem.at[0])

    @pl.when((j == nu) & (j % 2 == 0))
    def _():
        _wait_rows(h_hbm, xb, gsem.at[1])

    @pl.when((j == nu) & (j % 2 == 1))
    def _():
        _wait_rows(h_hbm, xa, gsem.at[0])


def _experts(h2, row_tok, row_dst, block_exp, n_used, w1, w3, w2):
    t, d = h2.shape
    nb = block_exp.shape[0]
    blk = MOE_BLOCK
    f = w1.shape[2]
    gidx = row_tok.reshape(nb, 1, blk)
    dump = 2 * t + jnp.arange(blk, dtype=jnp.int32)
    sidx = jnp.concatenate([dump, row_dst]).reshape(nb + 1, 1, blk)
    last = nb - 1
    smem = lambda m: pl.BlockSpec((1, 1, blk), m, memory_space=pltpu.SMEM)
    wspec = lambda shp: pl.BlockSpec(shp, lambda j, be, nu: (be[jnp.minimum(j, last)], 0, 0))
    grid_spec = pltpu.PrefetchScalarGridSpec(
        num_scalar_prefetch=2,
        grid=(nb + 1,),
        in_specs=[smem(lambda j, be, nu: (0, 0, 0)),
                  smem(lambda j, be, nu: (jnp.minimum(j + 1, last), 0, 0)),
                  smem(lambda j, be, nu: (j, 0, 0)),
                  pl.BlockSpec(memory_space=pl.ANY),
                  wspec((1, d, f)), wspec((1, d, f)), wspec((1, f, d))],
        out_specs=pl.BlockSpec(memory_space=pl.ANY),
        scratch_shapes=[pltpu.VMEM((blk, d), F32), pltpu.VMEM((blk, d), F32),
                        pltpu.VMEM((blk, d // 2), jnp.uint32),
                        pltpu.VMEM((blk, d // 2), jnp.uint32),
                        pltpu.SemaphoreType.DMA((2,)), pltpu.SemaphoreType.DMA((1,))],
    )
    return pl.pallas_call(
        _expert_kernel,
        grid_spec=grid_spec,
        out_shape=jax.ShapeDtypeStruct((2 * t + blk, d // 2), jnp.uint32),
        compiler_params=_params(("arbitrary",)),
        name="experts",
    )(block_exp, n_used, gidx, gidx, sidx, h2, w1, w3, w2)


def _final_kernel(p0_ref, p1_ref, xm_ref, w_ref, g2_ref, lg_ref, lb_ref, o_ref, *, alpha, rc):
    g2 = g2_ref[0]
    lg = lg_ref[...]
    lb = lb_ref[...]

    def body(r, carry):
        rows = pl.ds(pl.multiple_of(r * rc, rc), rc)
        w = w_ref[rows, :]
        a_lo, a_hi = _unpack_bf16_pairs(p0_ref[rows, :])
        b_lo, b_hi = _unpack_bf16_pairs(p1_ref[rows, :])
        y2 = jnp.concatenate([w[:, 0:1] * a_lo + w[:, 1:2] * b_lo,
                              w[:, 0:1] * a_hi + w[:, 1:2] * b_hi], axis=1)
        o_ref[rows, :] = _ln_rows(alpha * xm_ref[rows, :] + g2 * y2) * lg + lb
        return carry

    lax.fori_loop(0, xm_ref.shape[0] // rc, body, 0)


def _final(packed, x_mid, wts, gate2, l2g, l2b, seq, alpha, tm=512):
    t, d = x_mid.shape
    nt = t // tm
    kern = functools.partial(_final_kernel, alpha=alpha, rc=32)
    full = lambda a: pl.BlockSpec(a.shape, lambda i: (0,) * a.ndim)
    return pl.pallas_call(
        kern,
        grid=(nt,),
        in_specs=[pl.BlockSpec((tm, d // 2), lambda i: (i, 0)),
                  pl.BlockSpec((tm, d // 2), lambda i: (i + nt, 0)),
                  pl.BlockSpec((tm, d), lambda i: (i, 0)),
                  pl.BlockSpec((tm, 2), lambda i: (i, 0)),
                  pl.BlockSpec((1, 1, d), lambda i: ((i * tm) // seq, 0, 0)),
                  full(l2g), full(l2b)],
        out_specs=pl.BlockSpec((tm, d), lambda i: (i, 0)),
        out_shape=jax.ShapeDtypeStruct((t, d), F32),
        compiler_params=_params(("arbitrary",)),
        name="final",
    )(packed, packed, x_mid, wts, gate2, l2g, l2b)


def kernel(x, c, ctx, c_ctx, w_ada, b_ada, w_in, w_f_proj, lam_re, lam_im, log_dt, b_re, b_im,
           c_re, c_im, d_skip, w_glu, b_glu, w_s_proj, w_out, ln1_g, ln1_b, w_group, b_group,
           w_expert, b_expert, w1, w3, w2, ln2_g, ln2_b):
    depth = w_ada.shape[0]
    assert depth == 1, "single-layer block"
    bsz, seq, d = x.shape
    ctx_len = ctx.shape[1]
    t = bsz * seq
    fw = w_f_proj.shape[1]
    sw = w_s_proj.shape[1]
    gd = fw // F_GROUPS
    rows = seq // GRID_W
    alpha = (2.0 * depth) ** 0.25
    row2 = lambda v: v.reshape(1, -1).astype(F32)

    cc = jnp.concatenate([c, c_ctx[None], jnp.zeros((8 - bsz - 1, d), F32)], axis=0)
    mod = _adaln(cc, w_ada[0], b_ada[0].reshape(1, -1))
    mods = [mod[:, k * d:(k + 1) * d] for k in range(N_MOD)]
    per_b = lambda m: m[:bsz].reshape(bsz, 1, d)
    ctx_row = lambda m: m[bsz:bsz + 1].reshape(1, 1, d)

    wi = w_in[0]
    wi = jnp.concatenate([wi[:, fw + sw:], wi[:, :fw + sw]], axis=1).astype(BF16)
    tn = sw
    n_sig = (2 * d) // tn
    main, us = _ln_mod_matmul(x.reshape(t, d), per_b(mods[0]), per_b(mods[1]), wi,
                              rows_per_mod=seq, n_sig=n_sig, tm=1024, tn=tn)
    w_us = w_in[0][:, fw:fw + sw].astype(BF16)
    _, us_ctx = _ln_mod_matmul(ctx.reshape(bsz * ctx_len, d), ctx_row(mods[0]), ctx_row(mods[1]),
                               w_us, rows_per_mod=bsz * ctx_len, n_sig=0, tm=ctx_len, tn=tn)

    f_out = _fourier_mix(main, 2 * d, bsz, rows, GRID_W, gd)

    perm = _octet_perm()
    tables = _s5_tables(lam_re[0], lam_im[0], log_dt[0], b_re[0], b_im[0], c_re[0], c_im[0],
                        d_skip[0])
    zl = _relayout_in(us, perm)
    zc = _relayout_in(us_ctx, perm)
    y_chunks = _s5_core(zl, zc, tables, bsz)
    y_s = _relayout_out(y_chunks, perm)

    wr = jnp.concatenate([w_group[0], w_expert[0],
                          jnp.zeros((d, ROUTE_LANES - N_EXPERT_GROUPS - N_EXPERTS), F32)], axis=1)
    br = jnp.concatenate([b_group[0], b_expert[0],
                          jnp.zeros((ROUTE_LANES - N_EXPERT_GROUPS - N_EXPERTS,), F32)])
    x_mid, h2, logits = _merge(
        f_out, y_s, main, x.reshape(t, d), per_b(mods[2]), per_b(mods[3]), per_b(mods[4]),
        w_glu[0].astype(BF16), row2(b_glu[0]), w_f_proj[0].astype(BF16), w_s_proj[0].astype(BF16),
        w_out[0].astype(BF16), row2(ln1_g[0]), row2(ln1_b[0]), wr.astype(BF16), row2(br),
        seq, alpha)

    info, cnt = _route(logits)
    experts = info[:, 0:2].astype(jnp.int32)
    wts = info[:, 2:4]
    rank = info[:, 4:6].astype(jnp.int32)
    counts = cnt[0, :N_EXPERTS].astype(jnp.int32)
    blk = MOE_BLOCK
    n_assign = 2 * t
    nb = -(-(n_assign + N_EXPERTS * (blk - 1)) // blk)
    padded = (counts + blk - 1) // blk * blk
    pend = jnp.cumsum(padded)
    pstart = pend - padded
    dest = pstart[experts] + rank
    n_used = (pend[-1] // blk).astype(jnp.int32).reshape(1)
    block_row = jnp.arange(nb, dtype=jnp.int32) * blk
    block_exp = jnp.minimum(jnp.sum((pend[None, :] <= block_row[:, None]).astype(jnp.int32), axis=1),
                            N_EXPERTS - 1)
    assign = jnp.full((nb * blk,), -1, jnp.int32).at[dest.reshape(-1)].set(
        jnp.arange(n_assign, dtype=jnp.int32), unique_indices=True, mode="promise_in_bounds")
    valid = assign >= 0
    row_tok = jnp.where(valid, assign >> 1, 0)
    row_in_blk = jnp.arange(nb * blk, dtype=jnp.int32) % blk
    row_dst = jnp.where(valid, (assign & 1) * t + (assign >> 1), 2 * t + row_in_blk)

    packed = _experts(h2, row_tok, row_dst, block_exp, n_used, w1[0].astype(BF16),
                      w3[0].astype(BF16), w2[0].astype(BF16))
    out = _final(packed, x_mid, wts, per_b(mods[5]), row2(ln2_g[0]), row2(ln2_b[0]), seq, alpha)
    return out.reshape(bsz, seq, d)
```

```python
import functools
import math

import numpy as np
import jax
import jax.numpy as jnp
from jax import lax
from jax.experimental import pallas as pl
from jax.experimental.pallas import tpu as pltpu

F32 = jnp.float32
BF16 = jnp.bfloat16

GRID_W = 64
F_GROUPS = 4
S_GROUP_DIM = 16
S_STATE = 64
N_EXPERT_GROUPS = 4
EXPERTS_PER_GROUP = 8
N_EXPERTS = N_EXPERT_GROUPS * EXPERTS_PER_GROUP
N_MOD = 6
LN_EPS = 1e-5

LANES = 128
S5_CHUNK = 64
OCTET = LANES // S_GROUP_DIM
FOURIER_COL_GROUP = 8
MOE_BLOCK = 256
WEIGHT_CHUNK_ELEMS = 256 * 1024
WEIGHT_CHUNKS_IN_FLIGHT = 4
EXPERT_F_CHUNKS = 4
ROUTE_LANES = 128
VMEM_LIMIT = 56 * 1024 * 1024


def _params(sem, vmem=VMEM_LIMIT):
    return pltpu.CompilerParams(dimension_semantics=sem, vmem_limit_bytes=vmem)


def _ln_rows(x):
    mu = jnp.mean(x, axis=-1, keepdims=True)
    xc = x - mu
    var = jnp.mean(xc * xc, axis=-1, keepdims=True)
    return xc * lax.rsqrt(var + LN_EPS)


def _adaln_kernel(c_ref, w_ref, b_ref, o_ref):
    s = c_ref[...]
    s = s * jax.nn.sigmoid(s)
    o_ref[...] = jnp.dot(s.astype(BF16), w_ref[...].astype(BF16),
                         preferred_element_type=F32) + b_ref[...]


def _adaln(cc, w, b, tn=1024):
    m, d = cc.shape
    n = w.shape[1]
    return pl.pallas_call(
        _adaln_kernel,
        grid=(n // tn,),
        in_specs=[pl.BlockSpec((m, d), lambda j: (0, 0)),
                  pl.BlockSpec((d, tn), lambda j: (0, j)),
                  pl.BlockSpec((1, tn), lambda j: (0, j))],
        out_specs=pl.BlockSpec((m, tn), lambda j: (0, j)),
        out_shape=jax.ShapeDtypeStruct((m, n), F32),
        compiler_params=_params(("arbitrary",)),
        name="adaln",
    )(cc, w, b)


def _ln_mod_rows(x_ref, rows, sc1, sh):
    mu = jnp.mean(x_ref[rows, :], axis=-1, keepdims=True)
    xc = x_ref[rows, :] - mu
    var = jnp.mean(xc * xc, axis=-1, keepdims=True)
    return (x_ref[rows, :] - mu) * lax.rsqrt(var + LN_EPS) * sc1 + sh


def _lnmm_kernel(x0_ref, sh0_ref, sc0_ref, xn_ref, shn_ref, scn_ref, w_ref, o_ref, us_ref,
                 ha, hb, *, n_sig, nrc, mchunks):
    i = pl.program_id(0)
    j = pl.program_id(1)
    tm = xn_ref.shape[0]
    rcs = tm // nrc

    @pl.when((i == 0) & (j == 0))
    def _():
        def body(r, carry):
            rows = pl.ds(pl.multiple_of(r * rcs, rcs), rcs)
            ha[rows, :] = _ln_mod_rows(x0_ref, rows, 1.0 + sc0_ref[0], sh0_ref[0]).astype(BF16)
            return carry

        lax.fori_loop(0, nrc, body, 0)

    def step(h_cur, h_nxt):
        tmc = tm // mchunks
        for m in range(mchunks):
            rs = pl.ds(m * tmc, tmc)
            acc = jnp.dot(h_cur[rs, :], w_ref[...], preferred_element_type=F32)
            o_ref[rs, :] = jnp.where(j < n_sig, jax.nn.sigmoid(acc), acc).astype(o_ref.dtype)
            us_ref[rs, :] = acc
        c = jnp.minimum(j, nrc - 1)
        rows = pl.ds(pl.multiple_of(c * rcs, rcs), rcs)
        h_nxt[rows, :] = _ln_mod_rows(xn_ref, rows, 1.0 + scn_ref[0], shn_ref[0]).astype(BF16)

    @pl.when(i % 2 == 0)
    def _():
        step(ha, hb)

    @pl.when(i % 2 == 1)
    def _():
        step(hb, ha)


def _ln_mod_matmul(x2, shift, scale, w, rows_per_mod, n_sig, tm, tn):
    t, d = x2.shape
    n = w.shape[1]
    nt, nj = t // tm, n // tn
    nrc = min(8, nj)
    kern = functools.partial(_lnmm_kernel, n_sig=n_sig, nrc=nrc, mchunks=max(1, tm // 256))
    nxt = lambda i: jnp.minimum(i + 1, nt - 1)
    once = pl.Buffered(1)
    mod0 = pl.BlockSpec((1, 1, d), lambda i, j: (0, 0, 0), pipeline_mode=once)
    modn = pl.BlockSpec((1, 1, d), lambda i, j: ((nxt(i) * tm) // rows_per_mod, 0, 0))
    return pl.pallas_call(
        kern,
        grid=(nt, nj),
        in_specs=[pl.BlockSpec((tm, d), lambda i, j: (0, 0), pipeline_mode=once), mod0, mod0,
                  pl.BlockSpec((tm, d), lambda i, j: (nxt(i), 0)), modn, modn,
                  pl.BlockSpec((d, tn), lambda i, j: (0, j))],
        out_specs=[pl.BlockSpec((tm, tn), lambda i, j: (i, j)),
                   pl.BlockSpec((tm, tn), lambda i, j: (i, 0))],
        out_shape=[jax.ShapeDtypeStruct((t, n), BF16),
                   jax.ShapeDtypeStruct((t, tn), F32)],
        scratch_shapes=[pltpu.VMEM((tm, d), BF16), pltpu.VMEM((tm, d), BF16)],
        compiler_params=_params(("arbitrary", "arbitrary")),
        name="ln_mod_matmul",
    )(x2, shift, scale, x2, shift, scale, w)


def _dft_cs(n):
    k = np.arange(n)
    ang = 2.0 * np.pi * np.outer(k, k) / n
    return np.cos(ang), np.sin(ang)


def _fourier_tables(rows, cols, gd):
    c1, s1 = _dft_cs(gd)
    half = gd // 2
    w1 = np.stack([np.concatenate([c1[:, h * half:(h + 1) * half],
                                   -s1[:, h * half:(h + 1) * half]], axis=1)
                   for h in range(2)]) / math.sqrt(gd)
    c2, s2 = _dft_cs(cols)
    rep = 256 // cols
    eye = np.eye(rep)
    l2 = np.concatenate([np.kron(eye, c2), np.kron(eye, s2)], axis=0) / math.sqrt(cols)
    c3, s3 = _dft_cs(rows)
    m3 = np.concatenate([c3, s3], axis=0)
    return tuple(jnp.asarray(a, F32).astype(BF16) for a in (w1, l2, m3))


def _fourier_kernel(x_ref, w1_ref, l2_ref, m3_ref, o_ref, a_scr, b_scr, *, rows, cols, scale3):
    npos = rows * cols
    half = o_ref.shape[2]
    slab = l2_ref.shape[1]
    rps = slab // cols
    rc = 1024 if npos % 1024 == 0 else npos

    def s1(r, carry):
        rs = pl.ds(pl.multiple_of(r * rc, rc), rc)
        a_scr[rs, :] = jnp.dot(x_ref[rs, :], w1_ref[0],
                               preferred_element_type=F32).astype(BF16)
        return carry

    lax.fori_loop(0, npos // rc, s1, 0)

    unroll = 4 if (npos // slab) % 4 == 0 else 1

    def s2(s, carry):
        for k in range(unroll):
            si = s * unroll + k
            rs = pl.ds(pl.multiple_of(si * slab, slab), slab)
            g = jnp.dot(l2_ref[...], a_scr[rs, :], preferred_element_type=F32)
            rr = pl.ds(si * rps, rps)
            b_scr[0, rr] = (g[0:slab, 0:half] + g[slab:, half:]).reshape(rps, cols, half)
            b_scr[1, rr] = (g[0:slab, half:] - g[slab:, 0:half]).reshape(rps, cols, half)
        return carry

    lax.fori_loop(0, npos // slab // unroll, s2, 0)

    cg = FOURIER_COL_GROUP
    for c0 in range(0, cols, cg):
        xr = jnp.swapaxes(b_scr[0, :, c0:c0 + cg, :], 0, 1)
        xi = jnp.swapaxes(b_scr[1, :, c0:c0 + cg, :], 0, 1)
        bc = jnp.concatenate([p for k in range(cg) for p in (xr[k], xi[k])],
                             axis=1).astype(BF16)
        h = jnp.dot(m3_ref[...], bc, preferred_element_type=F32)
        ys = [h[0:rows, 2 * k * half:(2 * k + 1) * half]
              + h[rows:, (2 * k + 1) * half:(2 * k + 2) * half] for k in range(cg)]
        o_ref[:, c0:c0 + cg, :] = jnp.swapaxes(jnp.stack(ys, axis=0), 0, 1) * scale3


def _fourier_mix(u_all, col0, batch, rows, cols, gd):
    npos = rows * cols
    half = gd // 2
    w1, l2, m3 = _fourier_tables(rows, cols, gd)
    kern = functools.partial(_fourier_kernel, rows=rows, cols=cols, scale3=1.0 / math.sqrt(rows))
    cb0 = col0 // gd
    return pl.pallas_call(
        kern,
        grid=(batch, F_GROUPS, 2),
        in_specs=[pl.BlockSpec((npos, gd), lambda b, g, h: (b, cb0 + g)),
                  pl.BlockSpec((1, gd, gd), lambda b, g, h: (h, 0, 0)),
                  pl.BlockSpec(l2.shape, lambda b, g, h: (0, 0)),
                  pl.BlockSpec(m3.shape, lambda b, g, h: (0, 0))],
        out_specs=pl.BlockSpec((rows, cols, half), lambda b, g, h: (b, 0, g * 2 + h)),
        out_shape=jax.ShapeDtypeStruct((batch * rows, cols, F_GROUPS * gd), F32),
        scratch_shapes=[pltpu.VMEM((npos, gd), BF16), pltpu.VMEM((2, rows, cols, half), F32)],
        compiler_params=_params(("arbitrary", "arbitrary", "arbitrary")),
        name="fourier_mix",
    )(u_all, w1, l2, m3).reshape(batch * npos, F_GROUPS * gd)


def _octet_perm():
    n = OCTET * LANES
    p = np.zeros((n, n), np.float32)
    for k in range(OCTET):
        for g in range(OCTET):
            for h in range(S_GROUP_DIM):
                p[k * LANES + g * S_GROUP_DIM + h, g * LANES + k * S_GROUP_DIM + h] = 1.0
    return jnp.asarray(p, BF16)


def _relayout_in_kernel(x_ref, p_ref, o_ref, *, nchunk):
    io = pl.program_id(1)
    pieces = []
    for k in range(OCTET):
        pieces.append(x_ref[pl.ds(io * OCTET + k, nchunk, stride=S5_CHUNK), :].astype(BF16))
    a = jnp.concatenate(pieces, axis=1)
    b = jnp.dot(a, p_ref[...], preferred_element_type=F32).astype(BF16)
    for g in range(OCTET):
        o_ref[g] = b[:, g * LANES:(g + 1) * LANES]


def _relayout_in(us, perm):
    t, sw = us.shape
    nchunk = t // S5_CHUNK
    ngroups = sw // S_GROUP_DIM
    kern = functools.partial(_relayout_in_kernel, nchunk=nchunk)
    return pl.pallas_call(
        kern,
        grid=(sw // LANES, S5_CHUNK // OCTET),
        in_specs=[pl.BlockSpec((t, LANES), lambda o, io: (0, o)),
                  pl.BlockSpec(perm.shape, lambda o, io: (0, 0))],
        out_specs=pl.BlockSpec((OCTET, nchunk, LANES), lambda o, io: (o, 0, io)),
        out_shape=jax.ShapeDtypeStruct((ngroups, nchunk, S5_CHUNK * S_GROUP_DIM), BF16),
        compiler_params=_params(("arbitrary", "arbitrary")),
        name="s5_relayout_in",
    )(us, perm)


def _relayout_out_kernel(y_ref, p_ref, o_ref, *, nchunk):
    io = pl.program_id(1)
    b = jnp.concatenate([y_ref[g] for g in range(OCTET)], axis=1)
    a = jnp.dot(b, p_ref[...], preferred_element_type=F32)
    for k in range(OCTET):
        o_ref[pl.ds(io * OCTET + k, nchunk, stride=S5_CHUNK), :] = a[:, k * LANES:(k + 1) * LANES]


def _relayout_out(y, perm):
    ngroups, nchunk, _ = y.shape
    t = nchunk * S5_CHUNK
    sw = ngroups * S_GROUP_DIM
    kern = functools.partial(_relayout_out_kernel, nchunk=nchunk)
    return pl.pallas_call(
        kern,
        grid=(sw // LANES, S5_CHUNK // OCTET),
        in_specs=[pl.BlockSpec((OCTET, nchunk, LANES), lambda o, io: (o, 0, io)),
                  pl.BlockSpec(perm.shape, lambda o, io: (0, 0))],
        out_specs=pl.BlockSpec((t, LANES), lambda o, io: (0, o)),
        out_shape=jax.ShapeDtypeStruct((t, sw), F32),
        compiler_params=_params(("arbitrary", "arbitrary")),
        name="s5_relayout_out",
    )(y, perm)


def _s5_tables(lam_re, lam_im, log_dt, b_re, b_im, c_re, c_im, d_skip):
    lc = S5_CHUNK

    e_all = jnp.arange(-(lc - 1), lc + 1, dtype=F32)

    def direction(d):
        lr, li = lam_re[d].astype(F32), lam_im[d].astype(F32)
        dt = jnp.exp(log_dt[d].astype(F32))[:, None]
        zr, zi = lr * dt, li * dt
        mag = jnp.exp(zr)
        nr, ni = mag * jnp.cos(zi) - 1.0, mag * jnp.sin(zi)
        den = lr * lr + li * li
        qr, qi = (nr * lr + ni * li) / den, (ni * lr - nr * li) / den
        br, bi = b_re[d].astype(F32), b_im[d].astype(F32)
        bbr = qr[:, :, None] * br - qi[:, :, None] * bi
        bbi = qr[:, :, None] * bi + qi[:, :, None] * br
        bbar = (jnp.swapaxes(bbr, 1, 2), jnp.swapaxes(bbi, 1, 2))
        cc = (c_re[d].astype(F32), c_im[d].astype(F32))
        pm = jnp.exp(zr[:, None, :] * e_all[None, :, None])
        ang = zi[:, None, :] * e_all[None, :, None]
        return (pm * jnp.cos(ang), pm * jnp.sin(ang)), bbar, cc

    powf, bf, cf = direction(0)
    powb, bb, cb = direction(1)

    def up(t, lo):
        return (t[0][:, lo + lc - 1:lo + 2 * lc - 1], t[1][:, lo + lc - 1:lo + 2 * lc - 1])

    def down(t, hi):
        return (jnp.flip(t[0][:, hi:hi + lc], axis=1), jnp.flip(t[1][:, hi:hi + lc], axis=1))

    cat = lambda x, y: jnp.concatenate([x, y], axis=-1)
    dup = lambda t: [cat(t[0], t[0]), cat(t[1], t[1])]
    f_d0, f_u0, b_u0, b_d0 = down(powf, 0), up(powf, 0), up(powb, 0), down(powb, 0)
    f_dl, f_u1, b_dl = down(powf, lc - 1), up(powf, 1), down(powb, lc)
    pow_fac = jnp.stack(dup(f_d0) + dup(b_u0) + dup(f_u0) + dup(b_d0)
                        + [cat(f_dl[0], b_u0[0]), cat(f_dl[1], b_u0[1])]
                        + [cat(f_u1[0], b_dl[0]), cat(f_u1[1], b_dl[1])], axis=1)
    (bfr, bfi), (bbr, bbi) = bf, bb
    (cfr, cfi), (cbr, cbi) = cf, cb
    vec_fac = jnp.stack([cat(bfr, -bfi), cat(-bfi, -bfr),
                         cat(bbr, -bbi), cat(-bbi, -bbr),
                         cat(cfr, cfi), cat(-cfi, cfr),
                         cat(cbr, cbi), cat(-cbi, cbr),
                         cat(bfr, bbr), cat(-bfi, -bbi),
                         cat(bfi, bbi), cat(bfr, bbr),
                         cat(cfr, cbr), cat(-cfi, -cbi),
                         cat(-cfi, -cbi), cat(-cfr, -cbr)], axis=1)
    last = 2 * lc - 1
    decay = jnp.stack([jnp.concatenate([powf[0][:, last], powb[0][:, last]], axis=1),
                       jnp.concatenate([powf[1][:, last], powb[1][:, last]], axis=1)],
                      axis=1)
    dvec = jnp.tile(d_skip.astype(F32), (1, lc))[:, None, :]
    return pow_fac, vec_fac, decay, dvec


_NT = (((1,), (1,)), ((), ()))


def _s5_kernel(z_ref, zc_ref, pf_ref, vf_ref, dec_ref, dv_ref,
               y_ref, e_scr, ec_scr, srf_scr, srb_scr, sif_scr, sib_scr, *, batch, nlat, nctx):
    p2 = 2 * S_STATE
    z = z_ref[0]

    def table(ip, iv):
        pa, pb = pf_ref[0, ip], pf_ref[0, ip + 1]
        va, vb = vf_ref[0, iv], vf_ref[0, iv + 1]
        t = pa[:, None, :] * va[None, :, :] + pb[:, None, :] * vb[None, :, :]
        return t.reshape(pa.shape[0] * va.shape[0], pa.shape[1])

    pmat = jnp.concatenate([table(8, 8), table(8, 10)], axis=1).astype(BF16)
    e = jnp.dot(z, pmat, preferred_element_type=F32)
    ec = jnp.dot(zc_ref[0], pmat, preferred_element_type=F32)
    for col in range(2):
        e_scr[col] = e[:, col * p2:(col + 1) * p2]
        ec_scr[col] = ec[:, col * p2:(col + 1) * p2]

    ar = dec_ref[0, 0:1, :]
    ai = dec_ref[0, 1:2, :]
    lane = lax.broadcasted_iota(jnp.int32, (batch, p2), 1)
    is_f = lane < S_STATE

    def advance(sr, si, er, ei):
        return ar * sr - ai * si + er, ar * si + ai * sr + ei

    def pick(ref, nf, nb, n, col):
        f = ref[col, pl.ds(nf, batch, stride=n), :]
        b = ref[col, pl.ds(nb, batch, stride=n), :]
        return jnp.where(is_f, f, b)

    sr = jnp.zeros((batch, p2), F32)
    si = jnp.zeros((batch, p2), F32)
    for k in range(nctx):
        nf, nb = k, nctx - 1 - k
        sr, si = advance(sr, si, pick(ec_scr, nf, nb, nctx, 0), pick(ec_scr, nf, nb, nctx, 1))
    for k in range(nlat):
        nf, nb = k, nlat - 1 - k
        rf = pl.ds(nf, batch, stride=nlat)
        rb = pl.ds(nb, batch, stride=nlat)
        srf_scr[rf, :] = sr
        srb_scr[rb, :] = sr
        sif_scr[rf, :] = si
        sib_scr[rb, :] = si
        sr, si = advance(sr, si, pick(e_scr, nf, nb, nlat, 0), pick(e_scr, nf, nb, nlat, 1))
    lane_all = lax.broadcasted_iota(jnp.int32, srf_scr.shape, 1)
    s_re = jnp.where(lane_all < S_STATE, srf_scr[...], srb_scr[...])
    s_im = jnp.where(lane_all < S_STATE, sif_scr[...], sib_scr[...])
    s_all = jnp.concatenate([s_re, s_im], axis=1).astype(BF16)

    n = z.shape[1]
    tf = lax.dot_general(table(0, 0).astype(BF16), table(4, 4).astype(BF16), _NT,
                         preferred_element_type=F32)
    tb = lax.dot_general(table(2, 2).astype(BF16), table(6, 6).astype(BF16), _NT,
                         preferred_element_type=F32)
    ri = lax.broadcasted_iota(jnp.int32, (n, n), 0)
    ci = lax.broadcasted_iota(jnp.int32, (n, n), 1)
    rpos = ri // S_GROUP_DIM
    cpos = ci // S_GROUP_DIM
    tmat = (jnp.where(rpos <= cpos, tf, 0.0) + jnp.where(rpos >= cpos, tb, 0.0)
            + jnp.where(ri == ci, dv_ref[0], 0.0))
    y = jnp.dot(z, tmat.astype(BF16), preferred_element_type=F32)
    qmat_t = jnp.concatenate([table(10, 12), table(10, 14)], axis=1).astype(BF16)
    y = y + lax.dot_general(s_all, qmat_t, _NT, preferred_element_type=F32)
    y_ref[0] = y.astype(y_ref.dtype)


def _s5_core(zl, zc, tables, batch):
    pow_fac, vec_fac, decay, dvec = tables
    g, rl, n = zl.shape
    rcx = zc.shape[1]
    nlat, nctx = rl // batch, rcx // batch
    kern = functools.partial(_s5_kernel, batch=batch, nlat=nlat, nctx=nctx)
    gmap = lambda i: (i, 0, 0)
    spec = lambda a: pl.BlockSpec((1,) + a.shape[1:], lambda i: (i,) + (0,) * (a.ndim - 1))
    return pl.pallas_call(
        kern,
        grid=(g,),
        in_specs=[spec(zl), spec(zc), spec(pow_fac), spec(vec_fac), spec(decay), spec(dvec)],
        out_specs=pl.BlockSpec((1, rl, n), gmap),
        out_shape=jax.ShapeDtypeStruct((g, rl, n), BF16),
        scratch_shapes=[pltpu.VMEM((2, rl, 2 * S_STATE), F32),
                        pltpu.VMEM((2, rcx, 2 * S_STATE), F32)]
                       + [pltpu.VMEM((rl, 2 * S_STATE), F32)] * 4,
        compiler_params=_params(("arbitrary",)),
        name="s5_core",
    )(zl, zc, pow_fac, vec_fac, decay, dvec)


def _merge_kernel(fo_ref, ys_ref, zf_ref, zs_ref, x_ref, g1_ref, sh2_ref, sc2_ref,
                  wglu_ref, bglu_ref, wf_ref, ws_ref, wo_ref, l1g_ref, l1b_ref, wr_ref, br_ref,
                  xm_ref, h2_ref, lg_ref, mix_a, mix_b, h2_scr, *, alpha, rc):
    i = pl.program_id(0)

    @pl.when(i == 0)
    def _():
        mix_b[...] = jnp.zeros_like(mix_b)

    def step(mix_cur, mix_prv):
        y = jax.nn.gelu(ys_ref[...], approximate=True)
        glu = jnp.dot(y.astype(BF16), wglu_ref[...], preferred_element_type=F32) + bglu_ref[...]
        s_out = y * jax.nn.sigmoid(glu)
        a = jnp.dot(fo_ref[...].astype(BF16), wf_ref[...], preferred_element_type=F32)
        b = jnp.dot(s_out.astype(BF16), ws_ref[...], preferred_element_type=F32)
        m = zf_ref[...].astype(F32) * a + zs_ref[...].astype(F32) * b
        mix_cur[...] = jnp.dot(m.astype(BF16), wo_ref[...], preferred_element_type=F32)

        g1 = g1_ref[0]
        sh2 = sh2_ref[0]
        sc2 = 1.0 + sc2_ref[0]
        l1g = l1g_ref[...]
        l1b = l1b_ref[...]
        for r in range(x_ref.shape[0] // rc):
            rows = pl.ds(r * rc, rc)
            xm = _ln_rows(alpha * x_ref[rows, :] + g1 * mix_prv[rows, :]) * l1g + l1b
            xm_ref[rows, :] = xm
            h2 = _ln_rows(xm) * sc2 + sh2
            h2_ref[rows, :] = h2
            h2_scr[rows, :] = h2.astype(BF16)
        lg_ref[...] = jnp.dot(h2_scr[...], wr_ref[...], preferred_element_type=F32) + br_ref[...]

    @pl.when(i % 2 == 0)
    def _():
        step(mix_a, mix_b)

    @pl.when(i % 2 == 1)
    def _():
        step(mix_b, mix_a)


def _merge(f_out, y_s, main, x2, gate1, shift2, scale2, wglu, bglu, wf, ws, wo, l1g, l1b, wr, br,
           seq, alpha, tm=256):
    t, d = x2.shape
    nt = t // tm
    fw, sw = f_out.shape[1], y_s.shape[1]
    kern = functools.partial(_merge_kernel, alpha=alpha, rc=32)
    cur = lambda i: jnp.minimum(i, nt - 1)
    prv = lambda i: jnp.maximum(i - 1, 0)
    rowc = lambda w: pl.BlockSpec((tm, w), lambda i: (cur(i), 0))
    rowp = lambda w: pl.BlockSpec((tm, w), lambda i: (prv(i), 0))
    mod = pl.BlockSpec((1, 1, d), lambda i: ((prv(i) * tm) // seq, 0, 0))
    full = lambda a: pl.BlockSpec(a.shape, lambda i: (0,) * a.ndim, pipeline_mode=pl.Buffered(1))
    return pl.pallas_call(
        kern,
        grid=(nt + 1,),
        in_specs=[rowc(fw), rowc(sw),
                  pl.BlockSpec((tm, d), lambda i: (cur(i), 0)),
                  pl.BlockSpec((tm, d), lambda i: (cur(i), 1)),
                  rowp(d), mod, mod, mod,
                  full(wglu), full(bglu), full(wf), full(ws), full(wo), full(l1g), full(l1b),
                  full(wr), full(br)],
        out_specs=[rowp(d), rowp(d), rowp(ROUTE_LANES)],
        out_shape=[jax.ShapeDtypeStruct((t, d), F32), jax.ShapeDtypeStruct((t, d), F32),
                   jax.ShapeDtypeStruct((t, ROUTE_LANES), F32)],
        scratch_shapes=[pltpu.VMEM((tm, d), F32), pltpu.VMEM((tm, d), F32),
                        pltpu.VMEM((tm, d), BF16)],
        compiler_params=_params(("arbitrary",)),
        name="merge",
    )(f_out, y_s, main, main, x2, gate1, shift2, scale2, wglu, bglu, wf, ws, wo, l1g, l1b, wr, br)


def _route_kernel(lg_ref, tri_ref, info_ref, cnt_ref, run_scr):
    i = pl.program_id(0)

    @pl.when(i == 0)
    def _():
        run_scr[...] = jnp.zeros_like(run_scr)

    lg = lg_ref[...]
    tm = lg.shape[0]
    lane = lax.broadcasted_iota(jnp.int32, lg.shape, 1)
    neg = jnp.float32(-1e30)
    big = jnp.int32(ROUTE_LANES)
    ng, epg = N_EXPERT_GROUPS, EXPERTS_PER_GROUP

    is_g = lane < ng
    gl = jnp.where(is_g, lg, neg)
    gm = jnp.max(gl, axis=-1, keepdims=True)
    gidx = jnp.min(jnp.where(gl == gm, lane, big), axis=-1, keepdims=True)
    gsum = jnp.sum(jnp.where(is_g, jnp.exp(gl - gm), 0.0), axis=-1, keepdims=True)
    g_top = 1.0 / gsum

    lo = ng + gidx * epg
    el = jnp.where((lane >= lo) & (lane < lo + epg), lg, neg)
    m1 = jnp.max(el, axis=-1, keepdims=True)
    i1 = jnp.min(jnp.where(el == m1, lane, big), axis=-1, keepdims=True)
    el2 = jnp.where(lane == i1, neg, el)
    m2 = jnp.max(el2, axis=-1, keepdims=True)
    i2 = jnp.min(jnp.where(el2 == m2, lane, big), axis=-1, keepdims=True)
    tt = jnp.exp(m2 - m1)
    w0 = g_top / (1.0 + tt)
    w1 = g_top * tt / (1.0 + tt)
    e0 = i1 - ng
    e1 = i2 - ng

    oh = (lane == e0) | (lane == e1)
    pre = jnp.dot(tri_ref[...], jnp.where(oh, 1.0, 0.0).astype(BF16),
                  preferred_element_type=F32) + run_scr[...]
    r0 = jnp.sum(jnp.where(lane == e0, pre, 0.0), axis=-1, keepdims=True)
    r1 = jnp.sum(jnp.where(lane == e1, pre, 0.0), axis=-1, keepdims=True)
    run_scr[...] = run_scr[...] + jnp.sum(jnp.where(oh, 1.0, 0.0), axis=0, keepdims=True)
    cnt_ref[...] = run_scr[...]

    vals = (e0.astype(F32), e1.astype(F32), w0, w1, r0, r1)
    info = jnp.zeros(lg.shape, F32)
    for k, v in enumerate(vals):
        info = jnp.where(lane == k, v, info)
    info_ref[...] = info


def _route(logits, tm=512):
    t = logits.shape[0]
    tri = jnp.asarray(np.tril(np.ones((tm, tm), np.float32), -1), BF16)
    return pl.pallas_call(
        _route_kernel,
        grid=(t // tm,),
        in_specs=[pl.BlockSpec((tm, ROUTE_LANES), lambda i: (i, 0)),
                  pl.BlockSpec((tm, tm), lambda i: (0, 0))],
        out_specs=[pl.BlockSpec((tm, ROUTE_LANES), lambda i: (i, 0)),
                   pl.BlockSpec((1, ROUTE_LANES), lambda i: (0, 0))],
        out_shape=[jax.ShapeDtypeStruct((t, ROUTE_LANES), F32),
                   jax.ShapeDtypeStruct((1, ROUTE_LANES), F32)],
        scratch_shapes=[pltpu.VMEM((1, ROUTE_LANES), F32)],
        compiler_params=_params(("arbitrary",)),
        name="route",
    )(logits, tri)


def _gather_rows(src_hbm, idx_ref, dst, sem):
    for r in range(dst.shape[0]):
        pltpu.make_async_copy(src_hbm.at[pl.ds(idx_ref[0, 0, r], 1)],
                              dst.at[pl.ds(r, 1)], sem).start()


def _scatter_rows(src, idx_ref, dst_hbm, sem):
    for r in range(src.shape[0]):
        pltpu.make_async_copy(src.at[pl.ds(r, 1)],
                              dst_hbm.at[pl.ds(idx_ref[0, 0, r], 1)], sem).start()


def _wait_rows(any_hbm, buf, sem):
    for r in range(buf.shape[0]):
        pltpu.make_async_copy(any_hbm.at[pl.ds(0, 1)], buf.at[pl.ds(r, 1)], sem).wait()


def _expert_kernel(bexp_ref, nu_ref, wslot_ref, se_ref, sc0_ref, scn_ref,
                   g0_ref, g1_ref, g2_ref, s_ref, sl_ref, h_hbm, w1_hbm, w3_hbm, w2_hbm,
                   o_hbm, x3, y2, hm_scr, wb1, wb3, wb2, st_a, st_b, gsem, ssem, wsem):
    j = pl.program_id(0)
    nu = nu_ref[0]
    blk, d = x3.shape[1], x3.shape[2]
    f = wb1.shape[2]
    depth = st_a.shape[0]
    rows_a = st_a.shape[1]
    rows_b = st_b.shape[1]
    n_a, n_b = d // rows_a, f // rows_b
    n_chunks = 2 * n_a + n_b

    def chunk_copy(e, c, kind):
        slot = c % depth
        if kind == 0:
            return pltpu.make_async_copy(w1_hbm.at[e, pl.ds(c * rows_a, rows_a), :],
                                         st_a.at[slot], wsem.at[slot])
        if kind == 1:
            return pltpu.make_async_copy(w3_hbm.at[e, pl.ds((c - n_a) * rows_a, rows_a), :],
                                         st_a.at[slot], wsem.at[slot])
        return pltpu.make_async_copy(w2_hbm.at[e, pl.ds((c - 2 * n_a) * rows_b, rows_b), :],
                                     st_b.at[slot], wsem.at[depth + slot])

    def kind_is(c, kind):
        lo = (0, n_a, 2 * n_a)[kind]
        hi = (n_a, 2 * n_a, n_chunks)[kind]
        return (c >= lo) & (c < hi)

    def start_chunk(e, c):
        for kind in range(3):
            @pl.when(kind_is(c, kind))
            def _():
                chunk_copy(e, c, kind).start()

    def finish_chunk(e, c, tgt):
        slot = c % depth
        for kind in range(3):
            @pl.when(kind_is(c, kind))
            def _():
                chunk_copy(e, c, kind).wait()
                if kind == 0:
                    rs = pl.ds(pl.multiple_of(c * rows_a, rows_a), rows_a)
                    wb1[tgt, rs, :] = st_a[slot].astype(BF16)
                elif kind == 1:
                    rs = pl.ds(pl.multiple_of((c - n_a) * rows_a, rows_a), rows_a)
                    wb3[tgt, rs, :] = st_a[slot].astype(BF16)
                else:
                    rs = pl.ds(pl.multiple_of((c - 2 * n_a) * rows_b, rows_b), rows_b)
                    wb2[tgt, rs, :] = st_b[slot].astype(BF16)

    def stage(e, c0, cn, tgt):
        def it(k, carry):
            c = c0 + k

            @pl.when(c == 0)
            def _():
                for ahead in range(min(depth, n_chunks)):
                    start_chunk(e, c + ahead)

            finish_chunk(e, c, tgt)

            @pl.when(c + depth < n_chunks)
            def _():
                start_chunk(e, c + depth)

            return carry

        lax.fori_loop(0, cn, it, 0)

    @pl.when(j == 0)
    def _():
        y2[1] = jnp.zeros(y2.shape[1:], y2.dtype)
        dump = pltpu.make_async_copy(y2.at[1], o_hbm.at[pl.ds(o_hbm.shape[0] - 2 * blk, blk)],
                                     ssem.at[0])
        dump.start()
        dump.wait()
        stage(bexp_ref[0], 0, n_chunks, 0)
        _gather_rows(h_hbm, g0_ref, x3.at[0], gsem.at[0])
        _gather_rows(h_hbm, g1_ref, x3.at[1], gsem.at[1])

    @pl.when(j < nu)
    def _():
        xs = j % 3
        ys = j % 2
        ws = wslot_ref[j]
        nxt = (j + 2) % 3
        nq = EXPERT_F_CHUNKS
        fq, rq = f // nq, blk // nq
        _wait_rows(h_hbm, x3.at[xs], gsem.at[xs])

        x = x3[xs].astype(BF16)
        for q in range(nq):
            for r in range(q * rq, (q + 1) * rq):
                pltpu.make_async_copy(y2.at[1 - ys, pl.ds(r, 1)],
                                      o_hbm.at[pl.ds(s_ref[0, 0, r], 1)], ssem.at[1 - ys]).start()
                pltpu.make_async_copy(h_hbm.at[pl.ds(g2_ref[0, 0, r], 1)],
                                      x3.at[nxt, pl.ds(r, 1)], gsem.at[nxt]).start()
            fs = slice(q * fq, (q + 1) * fq)
            a = jnp.dot(x, wb1[ws, :, fs], preferred_element_type=F32)
            b = jnp.dot(x, wb3[ws, :, fs], preferred_element_type=F32)
            hm_scr[:, fs] = ((a * jax.nn.sigmoid(a)) * b).astype(BF16)

        @pl.when(j >= 1)
        def _():
            _wait_rows(o_hbm, y2.at[ys], ssem.at[ys])

        y2[ys] = jnp.dot(hm_scr[...], wb2[ws], preferred_element_type=F32)

        stage(se_ref[j], sc0_ref[j], scn_ref[j], 1 - ws)

        @pl.when(j == nu - 1)
        def _():
            _scatter_rows(y2.at[ys], sl_ref, o_hbm, ssem.at[ys])
            _wait_rows(o_hbm, y2.at[ys], ssem.at[ys])
            _wait_rows(o_hbm, y2.at[1 - ys], ssem.at[1 - ys])
            _wait_rows(h_hbm, x3.at[nxt], gsem.at[nxt])
            _wait_rows(h_hbm, x3.at[(j + 1) % 3], gsem.at[(j + 1) % 3])


def _expert_plan(block_exp, counts_padded, n_used, n_chunks, blk):
    ne = counts_padded.shape[0]
    nb = block_exp.shape[0]
    nblk = counts_padded // blk
    first = (jnp.cumsum(counts_padded) - counts_padded) // blk
    nonempty = nblk > 0
    ordinal = jnp.cumsum(nonempty.astype(jnp.int32)) - nonempty.astype(jnp.int32)
    cand = jnp.where(nonempty, jnp.arange(ne, dtype=jnp.int32), ne)
    suffix_min = jnp.flip(lax.cummin(jnp.flip(cand)))
    nxt = jnp.concatenate([suffix_min[1:], jnp.full((1,), ne, jnp.int32)])
    has_next = nxt < ne
    jj = jnp.arange(nb, dtype=jnp.int32)
    m = jj - first[block_exp]
    n_e = jnp.maximum(nblk[block_exp], 1)
    quota = (n_chunks + n_e - 1) // n_e
    c0 = jnp.clip(m * quota, 0, n_chunks)
    cn = jnp.clip(n_chunks - c0, 0, quota)
    cn = jnp.where(has_next[block_exp] & (jj < n_used[0]), cn, 0)
    stage_e = jnp.where(has_next[block_exp], nxt[block_exp], 0)
    wslot = ordinal[block_exp] % 2
    i32 = lambda v: v.astype(jnp.int32)
    return i32(wslot), i32(stage_e), i32(c0), i32(cn)


def _experts(h2, row_tok, row_dst, block_exp, counts_padded, n_used, w1, w3, w2):
    t, d = h2.shape
    nb = block_exp.shape[0]
    blk = MOE_BLOCK
    f = w1.shape[2]
    rows_a = min(d, max(8, WEIGHT_CHUNK_ELEMS // f))
    rows_b = min(f, max(8, WEIGHT_CHUNK_ELEMS // d))
    n_chunks = 2 * (d // rows_a) + f // rows_b
    wslot, stage_e, c0, cn = _expert_plan(block_exp, counts_padded, n_used, n_chunks, blk)
    gidx = row_tok.reshape(nb, 1, blk)
    dump1 = 2 * t + blk + jnp.arange(blk, dtype=jnp.int32)
    sidx = jnp.concatenate([dump1, row_dst]).reshape(nb + 1, 1, blk)
    last = nb - 1
    smem = lambda m: pl.BlockSpec((1, 1, blk), m, memory_space=pltpu.SMEM)
    anyspec = pl.BlockSpec(memory_space=pl.ANY)
    grid_spec = pltpu.PrefetchScalarGridSpec(
        num_scalar_prefetch=6,
        grid=(nb,),
        in_specs=[smem(lambda j, *_: (0, 0, 0)),
                  smem(lambda j, *_: (min(1, last), 0, 0)),
                  smem(lambda j, *_: (jnp.minimum(j + 2, last), 0, 0)),
                  smem(lambda j, *_: (j, 0, 0)),
                  smem(lambda j, *_: (j + 1, 0, 0)),
                  anyspec, anyspec, anyspec, anyspec],
        out_specs=anyspec,
        scratch_shapes=[pltpu.VMEM((3, blk, d), F32),
                        pltpu.VMEM((2, blk, d), F32),
                        pltpu.VMEM((blk, f), BF16),
                        pltpu.VMEM((2, d, f), BF16), pltpu.VMEM((2, d, f), BF16),
                        pltpu.VMEM((2, f, d), BF16),
                        pltpu.VMEM((WEIGHT_CHUNKS_IN_FLIGHT, rows_a, f), F32),
                        pltpu.VMEM((WEIGHT_CHUNKS_IN_FLIGHT, rows_b, d), F32),
                        pltpu.SemaphoreType.DMA((3,)), pltpu.SemaphoreType.DMA((2,)),
                        pltpu.SemaphoreType.DMA((2 * WEIGHT_CHUNKS_IN_FLIGHT,))],
    )
    return pl.pallas_call(
        _expert_kernel,
        grid_spec=grid_spec,
        out_shape=jax.ShapeDtypeStruct((2 * t + 2 * blk, d), F32),
        compiler_params=_params(("arbitrary",)),
        name="experts",
    )(block_exp, n_used, wslot, stage_e, c0, cn, gidx, gidx, gidx, sidx, sidx, h2, w1, w3, w2)


def _final_kernel(p0_ref, p1_ref, xm_ref, w_ref, g2_ref, lg_ref, lb_ref, o_ref, *, alpha, rc):
    g2 = g2_ref[0]
    lg = lg_ref[...]
    lb = lb_ref[...]

    for r in range(xm_ref.shape[0] // rc):
        rows = pl.ds(r * rc, rc)
        w = w_ref[rows, :]
        y2 = w[:, 0:1] * p0_ref[rows, :] + w[:, 1:2] * p1_ref[rows, :]
        o_ref[rows, :] = _ln_rows(alpha * xm_ref[rows, :] + g2 * y2) * lg + lb


def _final(packed, x_mid, wts, gate2, l2g, l2b, seq, alpha, tm=256):
    t, d = x_mid.shape
    nt = t // tm
    kern = functools.partial(_final_kernel, alpha=alpha, rc=32)
    full = lambda a: pl.BlockSpec(a.shape, lambda i: (0,) * a.ndim)
    return pl.pallas_call(
        kern,
        grid=(nt,),
        in_specs=[pl.BlockSpec((tm, d), lambda i: (i, 0)),
                  pl.BlockSpec((tm, d), lambda i: (i + nt, 0)),
                  pl.BlockSpec((tm, d), lambda i: (i, 0)),
                  pl.BlockSpec((tm, 2), lambda i: (i, 0)),
                  pl.BlockSpec((1, 1, d), lambda i: ((i * tm) // seq, 0, 0)),
                  full(l2g), full(l2b)],
        out_specs=pl.BlockSpec((tm, d), lambda i: (i, 0)),
        out_shape=jax.ShapeDtypeStruct((t, d), F32),
        compiler_params=_params(("arbitrary",)),
        name="final",
    )(packed, packed, x_mid, wts, gate2, l2g, l2b)


def kernel(x, c, ctx, c_ctx, w_ada, b_ada, w_in, w_f_proj, lam_re, lam_im, log_dt, b_re, b_im,
           c_re, c_im, d_skip, w_glu, b_glu, w_s_proj, w_out, ln1_g, ln1_b, w_group, b_group,
           w_expert, b_expert, w1, w3, w2, ln2_g, ln2_b):
    depth = w_ada.shape[0]
    assert depth == 1, "single-layer block"
    bsz, seq, d = x.shape
    ctx_len = ctx.shape[1]
    t = bsz * seq
    fw = w_f_proj.shape[1]
    sw = w_s_proj.shape[1]
    gd = fw // F_GROUPS
    rows = seq // GRID_W
    alpha = (2.0 * depth) ** 0.25
    row2 = lambda v: v.reshape(1, -1).astype(F32)

    cc = jnp.concatenate([c, c_ctx[None], jnp.zeros((8 - bsz - 1, d), F32)], axis=0)
    mod = _adaln(cc, w_ada[0], b_ada[0].reshape(1, -1))
    mods = [mod[:, k * d:(k + 1) * d] for k in range(N_MOD)]
    per_b = lambda m: m[:bsz].reshape(bsz, 1, d)
    ctx_row = lambda m: m[bsz:bsz + 1].reshape(1, 1, d)

    wi = w_in[0]
    wi = jnp.concatenate([wi[:, fw + sw:], wi[:, :fw + sw]], axis=1).astype(BF16)
    tn = sw
    n_sig = (2 * d) // tn
    main, us = _ln_mod_matmul(x.reshape(t, d), per_b(mods[0]), per_b(mods[1]), wi,
                              rows_per_mod=seq, n_sig=n_sig, tm=1024, tn=tn)
    w_us = w_in[0][:, fw:fw + sw].astype(BF16)
    _, us_ctx = _ln_mod_matmul(ctx.reshape(bsz * ctx_len, d), ctx_row(mods[0]), ctx_row(mods[1]),
                               w_us, rows_per_mod=bsz * ctx_len, n_sig=0, tm=ctx_len, tn=tn)

    f_out = _fourier_mix(main, 2 * d, bsz, rows, GRID_W, gd)

    perm = _octet_perm()
    tables = _s5_tables(lam_re[0], lam_im[0], log_dt[0], b_re[0], b_im[0], c_re[0], c_im[0],
                        d_skip[0])
    zl = _relayout_in(us, perm)
    zc = _relayout_in(us_ctx, perm)
    y_chunks = _s5_core(zl, zc, tables, bsz)
    y_s = _relayout_out(y_chunks, perm)

    wr = jnp.concatenate([w_group[0], w_expert[0],
                          jnp.zeros((d, ROUTE_LANES - N_EXPERT_GROUPS - N_EXPERTS), F32)], axis=1)
    br = jnp.concatenate([b_group[0], b_expert[0],
                          jnp.zeros((ROUTE_LANES - N_EXPERT_GROUPS - N_EXPERTS,), F32)])
    x_mid, h2, logits = _merge(
        f_out, y_s, main, x.reshape(t, d), per_b(mods[2]), per_b(mods[3]), per_b(mods[4]),
        w_glu[0].astype(BF16), row2(b_glu[0]), w_f_proj[0].astype(BF16), w_s_proj[0].astype(BF16),
        w_out[0].astype(BF16), row2(ln1_g[0]), row2(ln1_b[0]), wr.astype(BF16), row2(br),
        seq, alpha)

    info, cnt = _route(logits)
    experts = info[:, 0:2].astype(jnp.int32)
    wts = info[:, 2:4]
    rank = info[:, 4:6].astype(jnp.int32)
    counts = cnt[0, :N_EXPERTS].astype(jnp.int32)
    blk = MOE_BLOCK
    n_assign = 2 * t
    nb = -(-(n_assign + N_EXPERTS * (blk - 1)) // blk)
    padded = (counts + blk - 1) // blk * blk
    pend = jnp.cumsum(padded)
    pstart = pend - padded
    onehot = (experts[:, :, None] == jnp.arange(N_EXPERTS, dtype=jnp.int32)).astype(jnp.int32)
    dest = jnp.sum(onehot * pstart, axis=-1) + rank
    n_used = (pend[-1] // blk).astype(jnp.int32).reshape(1)
    block_row = jnp.arange(nb, dtype=jnp.int32) * blk
    block_exp = jnp.minimum(jnp.sum((pend[None, :] <= block_row[:, None]).astype(jnp.int32), axis=1),
                            N_EXPERTS - 1)
    assign = jnp.zeros((nb * blk,), jnp.int32).at[dest.reshape(-1)].add(
        jnp.arange(1, n_assign + 1, dtype=jnp.int32)) - 1
    valid = assign >= 0
    row_tok = jnp.where(valid, assign >> 1, 0)
    row_in_2blk = jnp.arange(nb * blk, dtype=jnp.int32) % (2 * blk)
    row_dst = jnp.where(valid, (assign & 1) * t + (assign >> 1), 2 * t + row_in_2blk)

    packed = _experts(h2, row_tok, row_dst, block_exp, padded, n_used, w1[0], w3[0], w2[0])
    out = _final(packed, x_mid, wts, per_b(mods[5]), row2(ln2_g[0]), row2(ln2_b[0]), seq, alpha)
    return out.reshape(bsz, seq, d)
```

```python
import functools
import math

import numpy as np
import jax
import jax.numpy as jnp
from jax import lax
from jax.experimental import pallas as pl
from jax.experimental.pallas import tpu as pltpu

F32 = jnp.float32
BF16 = jnp.bfloat16

GRID_W = 64
F_GROUPS = 4
S_GROUP_DIM = 16
S_STATE = 64
N_EXPERT_GROUPS = 4
EXPERTS_PER_GROUP = 8
N_EXPERTS = N_EXPERT_GROUPS * EXPERTS_PER_GROUP
N_MOD = 6
LN_EPS = 1e-5

LANES = 128
S5_CHUNK = 64
OCTET = LANES // S_GROUP_DIM
FOURIER_COL_GROUP = 8
MOE_BLOCK = 256
WEIGHT_CHUNK_ELEMS = 256 * 1024
WEIGHT_CHUNKS_IN_FLIGHT = 4
EXPERT_F_CHUNKS = 4
ROUTE_LANES = 128
VMEM_LIMIT = 56 * 1024 * 1024


def _params(sem, vmem=VMEM_LIMIT):
    return pltpu.CompilerParams(dimension_semantics=sem, vmem_limit_bytes=vmem)


def _ln_rows(x):
    mu = jnp.mean(x, axis=-1, keepdims=True)
    xc = x - mu
    var = jnp.mean(xc * xc, axis=-1, keepdims=True)
    return xc * lax.rsqrt(var + LN_EPS)


def _adaln_kernel(c_ref, w_ref, b_ref, o_ref):
    s = c_ref[...]
    s = s * jax.nn.sigmoid(s)
    o_ref[...] = jnp.dot(s.astype(BF16), w_ref[...].astype(BF16),
                         preferred_element_type=F32) + b_ref[...]


def _adaln(cc, w, b, tn=1024):
    m, d = cc.shape
    n = w.shape[1]
    return pl.pallas_call(
        _adaln_kernel,
        grid=(n // tn,),
        in_specs=[pl.BlockSpec((m, d), lambda j: (0, 0)),
                  pl.BlockSpec((d, tn), lambda j: (0, j)),
                  pl.BlockSpec((1, tn), lambda j: (0, j))],
        out_specs=pl.BlockSpec((m, tn), lambda j: (0, j)),
        out_shape=jax.ShapeDtypeStruct((m, n), F32),
        compiler_params=_params(("arbitrary",)),
        name="adaln",
    )(cc, w, b)


def _ln_mod_rows(x_ref, rows, sc1, sh):
    mu = jnp.mean(x_ref[rows, :], axis=-1, keepdims=True)
    xc = x_ref[rows, :] - mu
    var = jnp.mean(xc * xc, axis=-1, keepdims=True)
    return (x_ref[rows, :] - mu) * lax.rsqrt(var + LN_EPS) * sc1 + sh


def _lnmm_kernel(x0_ref, sh0_ref, sc0_ref, xn_ref, shn_ref, scn_ref, w_ref, o_ref, us_ref,
                 ha, hb, *, n_sig, nrc, mchunks):
    i = pl.program_id(0)
    j = pl.program_id(1)
    tm = xn_ref.shape[0]
    rcs = tm // nrc

    @pl.when((i == 0) & (j == 0))
    def _():
        def body(r, carry):
            rows = pl.ds(pl.multiple_of(r * rcs, rcs), rcs)
            ha[rows, :] = _ln_mod_rows(x0_ref, rows, 1.0 + sc0_ref[0], sh0_ref[0]).astype(BF16)
            return carry

        lax.fori_loop(0, nrc, body, 0)

    def step(h_cur, h_nxt):
        tmc = tm // mchunks
        for m in range(mchunks):
            rs = pl.ds(m * tmc, tmc)
            acc = jnp.dot(h_cur[rs, :], w_ref[...], preferred_element_type=F32)
            o_ref[rs, :] = jnp.where(j < n_sig, jax.nn.sigmoid(acc), acc).astype(o_ref.dtype)
            us_ref[rs, :] = acc
        c = jnp.minimum(j, nrc - 1)
        rows = pl.ds(pl.multiple_of(c * rcs, rcs), rcs)
        h_nxt[rows, :] = _ln_mod_rows(xn_ref, rows, 1.0 + scn_ref[0], shn_ref[0]).astype(BF16)

    @pl.when(i % 2 == 0)
    def _():
        step(ha, hb)

    @pl.when(i % 2 == 1)
    def _():
        step(hb, ha)


def _ln_mod_matmul(x2, shift, scale, w, rows_per_mod, n_sig, tm, tn, nj, col_shift):
    t, d = x2.shape
    nw = w.shape[1] // tn
    n = nj * tn
    nt = t // tm
    nrc = min(8, nj)
    kern = functools.partial(_lnmm_kernel, n_sig=n_sig, nrc=nrc, mchunks=max(1, tm // 256))
    nxt = lambda i: jnp.minimum(i + 1, nt - 1)
    once = pl.Buffered(1)
    mod0 = pl.BlockSpec((1, 1, d), lambda i, j: (0, 0, 0), pipeline_mode=once)
    modn = pl.BlockSpec((1, 1, d), lambda i, j: ((nxt(i) * tm) // rows_per_mod, 0, 0))
    return pl.pallas_call(
        kern,
        grid=(nt, nj),
        in_specs=[pl.BlockSpec((tm, d), lambda i, j: (0, 0), pipeline_mode=once), mod0, mod0,
                  pl.BlockSpec((tm, d), lambda i, j: (nxt(i), 0)), modn, modn,
                  pl.BlockSpec((d, tn), lambda i, j: (0, (j + col_shift) % nw))],
        out_specs=[pl.BlockSpec((tm, tn), lambda i, j: (i, j)),
                   pl.BlockSpec((tm, tn), lambda i, j: (i, 0))],
        out_shape=[jax.ShapeDtypeStruct((t, n), BF16),
                   jax.ShapeDtypeStruct((t, tn), F32)],
        scratch_shapes=[pltpu.VMEM((tm, d), BF16), pltpu.VMEM((tm, d), BF16)],
        compiler_params=_params(("arbitrary", "arbitrary")),
        name="ln_mod_matmul",
    )(x2, shift, scale, x2, shift, scale, w)


def _dft_cs(n):
    k = np.arange(n)
    ang = 2.0 * np.pi * np.outer(k, k) / n
    return np.cos(ang), np.sin(ang)


def _fourier_tables(rows, cols, gd):
    c1, s1 = _dft_cs(gd)
    half = gd // 2
    w1 = np.stack([np.concatenate([c1[:, h * half:(h + 1) * half],
                                   -s1[:, h * half:(h + 1) * half]], axis=1)
                   for h in range(2)]) / math.sqrt(gd)
    c2, s2 = _dft_cs(cols)
    rep = 256 // cols
    eye = np.eye(rep)
    l2 = np.concatenate([np.kron(eye, c2), np.kron(eye, s2)], axis=0) / math.sqrt(cols)
    c3, s3 = _dft_cs(rows)
    m3 = np.concatenate([c3, s3], axis=0)
    return tuple(jnp.asarray(a, F32).astype(BF16) for a in (w1, l2, m3))


def _fourier_kernel(x_ref, w1_ref, l2_ref, m3_ref, o_ref, a_scr, b_scr, *, rows, cols, scale3):
    npos = rows * cols
    half = o_ref.shape[2]
    slab = l2_ref.shape[1]
    rps = slab // cols
    rc = 1024 if npos % 1024 == 0 else npos

    def s1(r, carry):
        rs = pl.ds(pl.multiple_of(r * rc, rc), rc)
        a_scr[rs, :] = jnp.dot(x_ref[rs, :], w1_ref[0],
                               preferred_element_type=F32).astype(BF16)
        return carry

    lax.fori_loop(0, npos // rc, s1, 0)

    unroll = 4 if (npos // slab) % 4 == 0 else 1

    def s2(s, carry):
        for k in range(unroll):
            si = s * unroll + k
            rs = pl.ds(pl.multiple_of(si * slab, slab), slab)
            g = jnp.dot(l2_ref[...], a_scr[rs, :], preferred_element_type=F32)
            rr = pl.ds(si * rps, rps)
            b_scr[0, rr] = (g[0:slab, 0:half] + g[slab:, half:]).reshape(rps, cols, half)
            b_scr[1, rr] = (g[0:slab, half:] - g[slab:, 0:half]).reshape(rps, cols, half)
        return carry

    lax.fori_loop(0, npos // slab // unroll, s2, 0)

    cg = FOURIER_COL_GROUP
    for c0 in range(0, cols, cg):
        xr = jnp.swapaxes(b_scr[0, :, c0:c0 + cg, :], 0, 1)
        xi = jnp.swapaxes(b_scr[1, :, c0:c0 + cg, :], 0, 1)
        bc = jnp.concatenate([p for k in range(cg) for p in (xr[k], xi[k])],
                             axis=1).astype(BF16)
        h = jnp.dot(m3_ref[...], bc, preferred_element_type=F32)
        ys = [h[0:rows, 2 * k * half:(2 * k + 1) * half]
              + h[rows:, (2 * k + 1) * half:(2 * k + 2) * half] for k in range(cg)]
        o_ref[:, c0:c0 + cg, :] = jnp.swapaxes(jnp.stack(ys, axis=0), 0, 1) * scale3


def _fourier_mix(u_all, col0, batch, rows, cols, gd):
    npos = rows * cols
    half = gd // 2
    w1, l2, m3 = _fourier_tables(rows, cols, gd)
    kern = functools.partial(_fourier_kernel, rows=rows, cols=cols, scale3=1.0 / math.sqrt(rows))
    cb0 = col0 // gd
    return pl.pallas_call(
        kern,
        grid=(batch, F_GROUPS, 2),
        in_specs=[pl.BlockSpec((npos, gd), lambda b, g, h: (b, cb0 + g)),
                  pl.BlockSpec((1, gd, gd), lambda b, g, h: (h, 0, 0)),
                  pl.BlockSpec(l2.shape, lambda b, g, h: (0, 0)),
                  pl.BlockSpec(m3.shape, lambda b, g, h: (0, 0))],
        out_specs=pl.BlockSpec((rows, cols, half), lambda b, g, h: (b, 0, g * 2 + h)),
        out_shape=jax.ShapeDtypeStruct((batch * rows, cols, F_GROUPS * gd), F32),
        scratch_shapes=[pltpu.VMEM((npos, gd), BF16), pltpu.VMEM((2, rows, cols, half), F32)],
        compiler_params=_params(("arbitrary", "arbitrary", "arbitrary")),
        name="fourier_mix",
    )(u_all, w1, l2, m3).reshape(batch * npos, F_GROUPS * gd)


def _octet_perm():
    n = OCTET * LANES
    p = np.zeros((n, n), np.float32)
    for k in range(OCTET):
        for g in range(OCTET):
            for h in range(S_GROUP_DIM):
                p[k * LANES + g * S_GROUP_DIM + h, g * LANES + k * S_GROUP_DIM + h] = 1.0
    return jnp.asarray(p, BF16)


def _relayout_in_kernel(x_ref, p_ref, o_ref, *, nchunk):
    io = pl.program_id(1)
    pieces = []
    for k in range(OCTET):
        pieces.append(x_ref[pl.ds(io * OCTET + k, nchunk, stride=S5_CHUNK), :].astype(BF16))
    a = jnp.concatenate(pieces, axis=1)
    b = jnp.dot(a, p_ref[...], preferred_element_type=F32).astype(BF16)
    for g in range(OCTET):
        o_ref[g] = b[:, g * LANES:(g + 1) * LANES]


def _relayout_in(us, perm):
    t, sw = us.shape
    nchunk = t // S5_CHUNK
    ngroups = sw // S_GROUP_DIM
    kern = functools.partial(_relayout_in_kernel, nchunk=nchunk)
    return pl.pallas_call(
        kern,
        grid=(sw // LANES, S5_CHUNK // OCTET),
        in_specs=[pl.BlockSpec((t, LANES), lambda o, io: (0, o)),
                  pl.BlockSpec(perm.shape, lambda o, io: (0, 0))],
        out_specs=pl.BlockSpec((OCTET, nchunk, LANES), lambda o, io: (o, 0, io)),
        out_shape=jax.ShapeDtypeStruct((ngroups, nchunk, S5_CHUNK * S_GROUP_DIM), BF16),
        compiler_params=_params(("arbitrary", "arbitrary")),
        name="s5_relayout_in",
    )(us, perm)


def _relayout_out_kernel(y_ref, p_ref, o_ref, *, nchunk):
    io = pl.program_id(1)
    b = jnp.concatenate([y_ref[g] for g in range(OCTET)], axis=1)
    a = jnp.dot(b, p_ref[...], preferred_element_type=F32)
    for k in range(OCTET):
        o_ref[pl.ds(io * OCTET + k, nchunk, stride=S5_CHUNK), :] = a[:, k * LANES:(k + 1) * LANES]


def _relayout_out(y, perm):
    ngroups, nchunk, _ = y.shape
    t = nchunk * S5_CHUNK
    sw = ngroups * S_GROUP_DIM
    kern = functools.partial(_relayout_out_kernel, nchunk=nchunk)
    return pl.pallas_call(
        kern,
        grid=(sw // LANES, S5_CHUNK // OCTET),
        in_specs=[pl.BlockSpec((OCTET, nchunk, LANES), lambda o, io: (o, 0, io)),
                  pl.BlockSpec(perm.shape, lambda o, io: (0, 0))],
        out_specs=pl.BlockSpec((t, LANES), lambda o, io: (0, o)),
        out_shape=jax.ShapeDtypeStruct((t, sw), F32),
        compiler_params=_params(("arbitrary", "arbitrary")),
        name="s5_relayout_out",
    )(y, perm)


def _s5_tables(lam_re, lam_im, log_dt, b_re, b_im, c_re, c_im, d_skip):
    lc = S5_CHUNK

    e_all = jnp.arange(-(lc - 1), lc + 1, dtype=F32)

    def direction(d):
        lr, li = lam_re[d].astype(F32), lam_im[d].astype(F32)
        dt = jnp.exp(log_dt[d].astype(F32))[:, None]
        zr, zi = lr * dt, li * dt
        mag = jnp.exp(zr)
        nr, ni = mag * jnp.cos(zi) - 1.0, mag * jnp.sin(zi)
        den = lr * lr + li * li
        qr, qi = (nr * lr + ni * li) / den, (ni * lr - nr * li) / den
        br, bi = b_re[d].astype(F32), b_im[d].astype(F32)
        bbr = qr[:, :, None] * br - qi[:, :, None] * bi
        bbi = qr[:, :, None] * bi + qi[:, :, None] * br
        bbar = (jnp.swapaxes(bbr, 1, 2), jnp.swapaxes(bbi, 1, 2))
        cc = (c_re[d].astype(F32), c_im[d].astype(F32))
        pm = jnp.exp(zr[:, None, :] * e_all[None, :, None])
        ang = zi[:, None, :] * e_all[None, :, None]
        return (pm * jnp.cos(ang), pm * jnp.sin(ang)), bbar, cc

    powf, bf, cf = direction(0)
    powb, bb, cb = direction(1)

    def up(t, lo):
        return (t[0][:, lo + lc - 1:lo + 2 * lc - 1], t[1][:, lo + lc - 1:lo + 2 * lc - 1])

    def down(t, hi):
        return (jnp.flip(t[0][:, hi:hi + lc], axis=1), jnp.flip(t[1][:, hi:hi + lc], axis=1))

    cat = lambda x, y: jnp.concatenate([x, y], axis=-1)
    dup = lambda t: [cat(t[0], t[0]), cat(t[1], t[1])]
    f_d0, f_u0, b_u0, b_d0 = down(powf, 0), up(powf, 0), up(powb, 0), down(powb, 0)
    f_dl, f_u1, b_dl = down(powf, lc - 1), up(powf, 1), down(powb, lc)
    pow_fac = jnp.stack(dup(f_d0) + dup(b_u0) + dup(f_u0) + dup(b_d0)
                        + [cat(f_dl[0], b_u0[0]), cat(f_dl[1], b_u0[1])]
                        + [cat(f_u1[0], b_dl[0]), cat(f_u1[1], b_dl[1])], axis=0)
    (bfr, bfi), (bbr, bbi) = bf, bb
    (cfr, cfi), (cbr, cbi) = cf, cb
    vec_fac = jnp.stack([cat(bfr, -bfi), cat(-bfi, -bfr),
                         cat(bbr, -bbi), cat(-bbi, -bbr),
                         cat(cfr, cfi), cat(-cfi, cfr),
                         cat(cbr, cbi), cat(-cbi, cbr),
                         cat(bfr, bbr), cat(-bfi, -bbi),
                         cat(bfi, bbi), cat(bfr, bbr),
                         cat(cfr, cbr), cat(-cfi, -cbi),
                         cat(-cfi, -cbi), cat(-cfr, -cbr)], axis=0)
    last = 2 * lc - 1
    decay = jnp.stack([jnp.concatenate([powf[0][:, last], powb[0][:, last]], axis=1),
                       jnp.concatenate([powf[1][:, last], powb[1][:, last]], axis=1)],
                      axis=1)
    dvec = jnp.tile(d_skip.astype(F32), (1, lc))[:, None, :]
    return pow_fac, vec_fac, decay, dvec


_NT = (((1,), (1,)), ((), ()))


def _s5_kernel(z_ref, zc_ref, pf_ref, vf_ref, dec_ref, dv_ref,
               y_ref, e_scr, ec_scr, srf_scr, srb_scr, sif_scr, sib_scr, *, batch, nlat, nctx):
    p2 = 2 * S_STATE
    z = z_ref[0]

    def table(ip, iv):
        pa, pb = pf_ref[ip, 0], pf_ref[ip + 1, 0]
        va, vb = vf_ref[iv, 0], vf_ref[iv + 1, 0]
        t = pa[:, None, :] * va[None, :, :] + pb[:, None, :] * vb[None, :, :]
        return t.reshape(pa.shape[0] * va.shape[0], pa.shape[1])

    pmat = jnp.concatenate([table(8, 8), table(8, 10)], axis=1).astype(BF16)
    e = jnp.dot(z, pmat, preferred_element_type=F32)
    ec = jnp.dot(zc_ref[0], pmat, preferred_element_type=F32)
    for col in range(2):
        e_scr[col] = e[:, col * p2:(col + 1) * p2]
        ec_scr[col] = ec[:, col * p2:(col + 1) * p2]

    ar = dec_ref[0, 0:1, :]
    ai = dec_ref[0, 1:2, :]
    lane = lax.broadcasted_iota(jnp.int32, (batch, p2), 1)
    is_f = lane < S_STATE

    def advance(sr, si, er, ei):
        return ar * sr - ai * si + er, ar * si + ai * sr + ei

    def pick(ref, nf, nb, n, col):
        f = ref[col, pl.ds(nf, batch, stride=n), :]
        b = ref[col, pl.ds(nb, batch, stride=n), :]
        return jnp.where(is_f, f, b)

    sr = jnp.zeros((batch, p2), F32)
    si = jnp.zeros((batch, p2), F32)
    for k in range(nctx):
        nf, nb = k, nctx - 1 - k
        sr, si = advance(sr, si, pick(ec_scr, nf, nb, nctx, 0), pick(ec_scr, nf, nb, nctx, 1))
    for k in range(nlat):
        nf, nb = k, nlat - 1 - k
        rf = pl.ds(nf, batch, stride=nlat)
        rb = pl.ds(nb, batch, stride=nlat)
        srf_scr[rf, :] = sr
        srb_scr[rb, :] = sr
        sif_scr[rf, :] = si
        sib_scr[rb, :] = si
        sr, si = advance(sr, si, pick(e_scr, nf, nb, nlat, 0), pick(e_scr, nf, nb, nlat, 1))
    lane_all = lax.broadcasted_iota(jnp.int32, srf_scr.shape, 1)
    s_re = jnp.where(lane_all < S_STATE, srf_scr[...], srb_scr[...])
    s_im = jnp.where(lane_all < S_STATE, sif_scr[...], sib_scr[...])
    s_all = jnp.concatenate([s_re, s_im], axis=1).astype(BF16)

    n = z.shape[1]
    tf = lax.dot_general(table(0, 0).astype(BF16), table(4, 4).astype(BF16), _NT,
                         preferred_element_type=F32)
    tb = lax.dot_general(table(2, 2).astype(BF16), table(6, 6).astype(BF16), _NT,
                         preferred_element_type=F32)
    ri = lax.broadcasted_iota(jnp.int32, (n, n), 0)
    ci = lax.broadcasted_iota(jnp.int32, (n, n), 1)
    rpos = ri // S_GROUP_DIM
    cpos = ci // S_GROUP_DIM
    tmat = (jnp.where(rpos <= cpos, tf, 0.0) + jnp.where(rpos >= cpos, tb, 0.0)
            + jnp.where(ri == ci, dv_ref[0], 0.0))
    y = jnp.dot(z, tmat.astype(BF16), preferred_element_type=F32)
    qmat_t = jnp.concatenate([table(10, 12), table(10, 14)], axis=1).astype(BF16)
    y = y + lax.dot_general(s_all, qmat_t, _NT, preferred_element_type=F32)
    y_ref[0] = y.astype(y_ref.dtype)


def _s5_core(zl, zc, tables, batch):
    pow_fac, vec_fac, decay, dvec = tables
    g, rl, n = zl.shape
    rcx = zc.shape[1]
    nlat, nctx = rl // batch, rcx // batch
    kern = functools.partial(_s5_kernel, batch=batch, nlat=nlat, nctx=nctx)
    gmap = lambda i: (i, 0, 0)
    spec = lambda a: pl.BlockSpec((1,) + a.shape[1:], lambda i: (i,) + (0,) * (a.ndim - 1))
    fac_spec = lambda a: pl.BlockSpec((a.shape[0], 1) + a.shape[2:], lambda i: (0, i, 0, 0))
    return pl.pallas_call(
        kern,
        grid=(g,),
        in_specs=[spec(zl), spec(zc), fac_spec(pow_fac), fac_spec(vec_fac), spec(decay), spec(dvec)],
        out_specs=pl.BlockSpec((1, rl, n), gmap),
        out_shape=jax.ShapeDtypeStruct((g, rl, n), BF16),
        scratch_shapes=[pltpu.VMEM((2, rl, 2 * S_STATE), F32),
                        pltpu.VMEM((2, rcx, 2 * S_STATE), F32)]
                       + [pltpu.VMEM((rl, 2 * S_STATE), F32)] * 4,
        compiler_params=_params(("arbitrary",)),
        name="s5_core",
    )(zl, zc, pow_fac, vec_fac, decay, dvec)


def _merge_kernel(fo_ref, ys_ref, zf_ref, zs_ref, x_ref, g1_ref, sh2_ref, sc2_ref,
                  wglu_ref, bglu_ref, wf_ref, ws_ref, wo_ref, l1g_ref, l1b_ref, wr_ref, br_ref,
                  xm_ref, h2_ref, lg_ref, mix_a, mix_b, h2_scr, *, alpha, rc):
    i = pl.program_id(0)

    @pl.when(i == 0)
    def _():
        mix_b[...] = jnp.zeros_like(mix_b)

    def step(mix_cur, mix_prv):
        y = jax.nn.gelu(ys_ref[...], approximate=True)
        glu = jnp.dot(y.astype(BF16), wglu_ref[...], preferred_element_type=F32) + bglu_ref[...]
        s_out = y * jax.nn.sigmoid(glu)
        a = jnp.dot(fo_ref[...].astype(BF16), wf_ref[...], preferred_element_type=F32)
        b = jnp.dot(s_out.astype(BF16), ws_ref[...], preferred_element_type=F32)
        m = zf_ref[...].astype(F32) * a + zs_ref[...].astype(F32) * b
        mix_cur[...] = jnp.dot(m.astype(BF16), wo_ref[...], preferred_element_type=F32)

        g1 = g1_ref[0]
        sh2 = sh2_ref[0]
        sc2 = 1.0 + sc2_ref[0]
        l1g = l1g_ref[...]
        l1b = l1b_ref[...]
        for r in range(x_ref.shape[0] // rc):
            rows = pl.ds(r * rc, rc)
            xm = _ln_rows(alpha * x_ref[rows, :] + g1 * mix_prv[rows, :]) * l1g + l1b
            xm_ref[rows, :] = xm
            h2 = _ln_rows(xm) * sc2 + sh2
            h2_ref[rows, :] = h2
            h2_scr[rows, :] = h2.astype(BF16)
        lg_ref[...] = jnp.dot(h2_scr[...], wr_ref[...], preferred_element_type=F32) + br_ref[...]

    @pl.when(i % 2 == 0)
    def _():
        step(mix_a, mix_b)

    @pl.when(i % 2 == 1)
    def _():
        step(mix_b, mix_a)


def _merge(f_out, y_s, main, x2, gate1, shift2, scale2, wglu, bglu, wf, ws, wo, l1g, l1b, wr, br,
           seq, alpha, tm=256):
    t, d = x2.shape
    nt = t // tm
    fw, sw = f_out.shape[1], y_s.shape[1]
    kern = functools.partial(_merge_kernel, alpha=alpha, rc=32)
    cur = lambda i: jnp.minimum(i, nt - 1)
    prv = lambda i: jnp.maximum(i - 1, 0)
    rowc = lambda w: pl.BlockSpec((tm, w), lambda i: (cur(i), 0))
    rowp = lambda w: pl.BlockSpec((tm, w), lambda i: (prv(i), 0))
    mod = pl.BlockSpec((1, 1, d), lambda i: ((prv(i) * tm) // seq, 0, 0))
    full = lambda a: pl.BlockSpec(a.shape, lambda i: (0,) * a.ndim, pipeline_mode=pl.Buffered(1))
    return pl.pallas_call(
        kern,
        grid=(nt + 1,),
        in_specs=[rowc(fw), rowc(sw),
                  pl.BlockSpec((tm, d), lambda i: (cur(i), 0)),
                  pl.BlockSpec((tm, d), lambda i: (cur(i), 1)),
                  rowp(d), mod, mod, mod,
                  full(wglu), full(bglu), full(wf), full(ws), full(wo), full(l1g), full(l1b),
                  full(wr), full(br)],
        out_specs=[rowp(d), rowp(d), rowp(ROUTE_LANES)],
        out_shape=[jax.ShapeDtypeStruct((t, d), F32), jax.ShapeDtypeStruct((t, d), F32),
                   jax.ShapeDtypeStruct((t, ROUTE_LANES), F32)],
        scratch_shapes=[pltpu.VMEM((tm, d), F32), pltpu.VMEM((tm, d), F32),
                        pltpu.VMEM((tm, d), BF16)],
        compiler_params=_params(("arbitrary",)),
        name="merge",
    )(f_out, y_s, main, main, x2, gate1, shift2, scale2, wglu, bglu, wf, ws, wo, l1g, l1b, wr, br)


def _route_kernel(lg_ref, tri_ref, idx_ref, w_ref, cnt_ref, run_scr):
    i = pl.program_id(0)

    @pl.when(i == 0)
    def _():
        run_scr[...] = jnp.zeros_like(run_scr)

    lg = lg_ref[...]
    tm = lg.shape[0]
    lane = lax.broadcasted_iota(jnp.int32, lg.shape, 1)
    neg = jnp.float32(-1e30)
    big = jnp.int32(ROUTE_LANES)
    ng, epg = N_EXPERT_GROUPS, EXPERTS_PER_GROUP

    is_g = lane < ng
    gl = jnp.where(is_g, lg, neg)
    gm = jnp.max(gl, axis=-1, keepdims=True)
    gidx = jnp.min(jnp.where(gl == gm, lane, big), axis=-1, keepdims=True)
    gsum = jnp.sum(jnp.where(is_g, jnp.exp(gl - gm), 0.0), axis=-1, keepdims=True)
    g_top = 1.0 / gsum

    lo = ng + gidx * epg
    el = jnp.where((lane >= lo) & (lane < lo + epg), lg, neg)
    m1 = jnp.max(el, axis=-1, keepdims=True)
    i1 = jnp.min(jnp.where(el == m1, lane, big), axis=-1, keepdims=True)
    el2 = jnp.where(lane == i1, neg, el)
    m2 = jnp.max(el2, axis=-1, keepdims=True)
    i2 = jnp.min(jnp.where(el2 == m2, lane, big), axis=-1, keepdims=True)
    tt = jnp.exp(m2 - m1)
    w0 = g_top / (1.0 + tt)
    w1 = g_top * tt / (1.0 + tt)
    e0 = i1 - ng
    e1 = i2 - ng

    oh = (lane == e0) | (lane == e1)
    pre = jnp.dot(tri_ref[...], jnp.where(oh, 1.0, 0.0).astype(BF16),
                  preferred_element_type=F32) + run_scr[...]
    r0 = jnp.sum(jnp.where(lane == e0, pre, 0.0), axis=-1, keepdims=True)
    r1 = jnp.sum(jnp.where(lane == e1, pre, 0.0), axis=-1, keepdims=True)
    run_scr[...] = run_scr[...] + jnp.sum(jnp.where(oh, 1.0, 0.0), axis=0, keepdims=True)
    cnt_ref[...] = run_scr[...]

    vals = (e0.astype(F32), e1.astype(F32), r0, r1)
    info = jnp.zeros(lg.shape, F32)
    for k, v in enumerate(vals):
        info = jnp.where(lane == k, v, info)
    sel = (lax.broadcasted_iota(jnp.int32, (8, ROUTE_LANES), 0)
           == lax.broadcasted_iota(jnp.int32, (8, ROUTE_LANES), 1)).astype(F32)
    idx_ref[...] = lax.dot_general(sel, info, _NT, precision=lax.Precision.HIGHEST,
                                   preferred_element_type=F32)
    lane2 = lax.broadcasted_iota(jnp.int32, (tm, 2), 1)
    w_ref[...] = jnp.where(lane2 == 0, w0, w1)


def _route(logits, tm=512):
    t = logits.shape[0]
    tri = jnp.asarray(np.tril(np.ones((tm, tm), np.float32), -1), BF16)
    return pl.pallas_call(
        _route_kernel,
        grid=(t // tm,),
        in_specs=[pl.BlockSpec((tm, ROUTE_LANES), lambda i: (i, 0)),
                  pl.BlockSpec((tm, tm), lambda i: (0, 0))],
        out_specs=[pl.BlockSpec((8, tm), lambda i: (0, i)),
                   pl.BlockSpec((tm, 2), lambda i: (i, 0)),
                   pl.BlockSpec((1, ROUTE_LANES), lambda i: (0, 0))],
        out_shape=[jax.ShapeDtypeStruct((8, t), F32),
                   jax.ShapeDtypeStruct((t, 2), F32),
                   jax.ShapeDtypeStruct((1, ROUTE_LANES), F32)],
        scratch_shapes=[pltpu.VMEM((1, ROUTE_LANES), F32)],
        compiler_params=_params(("arbitrary",)),
        name="route",
    )(logits, tri)


def _gather_rows(src_hbm, idx_ref, dst, sem):
    for r in range(dst.shape[0]):
        pltpu.make_async_copy(src_hbm.at[pl.ds(idx_ref[0, 0, r], 1)],
                              dst.at[pl.ds(r, 1)], sem).start()


def _scatter_rows(src, idx_ref, dst_hbm, sem):
    for r in range(src.shape[0]):
        pltpu.make_async_copy(src.at[pl.ds(r, 1)],
                              dst_hbm.at[pl.ds(idx_ref[0, 0, r], 1)], sem).start()


def _wait_rows(any_hbm, buf, sem):
    for r in range(buf.shape[0]):
        pltpu.make_async_copy(any_hbm.at[pl.ds(0, 1)], buf.at[pl.ds(r, 1)], sem).wait()


def _expert_kernel(bexp_ref, nu_ref, wslot_ref, se_ref, sc0_ref, scn_ref,
                   g0_ref, g1_ref, g2_ref, s_ref, sl_ref, h_hbm, w1_hbm, w3_hbm, w2_hbm,
                   o_hbm, x3, y2, hm_scr, wb1, wb3, wb2, st_a, st_b, gsem, ssem, wsem):
    j = pl.program_id(0)
    nu = nu_ref[0]
    blk, d = x3.shape[1], x3.shape[2]
    f = wb1.shape[2]
    depth = st_a.shape[0]
    rows_a = st_a.shape[1]
    rows_b = st_b.shape[1]
    n_a, n_b = d // rows_a, f // rows_b
    n_chunks = 2 * n_a + n_b

    def chunk_copy(e, c, kind):
        slot = c % depth
        if kind == 0:
            return pltpu.make_async_copy(w1_hbm.at[e, pl.ds(c * rows_a, rows_a), :],
                                         st_a.at[slot], wsem.at[slot])
        if kind == 1:
            return pltpu.make_async_copy(w3_hbm.at[e, pl.ds((c - n_a) * rows_a, rows_a), :],
                                         st_a.at[slot], wsem.at[slot])
        return pltpu.make_async_copy(w2_hbm.at[e, pl.ds((c - 2 * n_a) * rows_b, rows_b), :],
                                     st_b.at[slot], wsem.at[depth + slot])

    def kind_is(c, kind):
        lo = (0, n_a, 2 * n_a)[kind]
        hi = (n_a, 2 * n_a, n_chunks)[kind]
        return (c >= lo) & (c < hi)

    def start_chunk(e, c):
        for kind in range(3):
            @pl.when(kind_is(c, kind))
            def _():
                chunk_copy(e, c, kind).start()

    def finish_chunk(e, c, tgt):
        slot = c % depth
        for kind in range(3):
            @pl.when(kind_is(c, kind))
            def _():
                chunk_copy(e, c, kind).wait()
                if kind == 0:
                    rs = pl.ds(pl.multiple_of(c * rows_a, rows_a), rows_a)
                    wb1[tgt, rs, :] = st_a[slot].astype(BF16)
                elif kind == 1:
                    rs = pl.ds(pl.multiple_of((c - n_a) * rows_a, rows_a), rows_a)
                    wb3[tgt, rs, :] = st_a[slot].astype(BF16)
                else:
                    rs = pl.ds(pl.multiple_of((c - 2 * n_a) * rows_b, rows_b), rows_b)
                    wb2[tgt, rs, :] = st_b[slot].astype(BF16)

    def stage(e, c0, cn, tgt):
        def it(k, carry):
            c = c0 + k

            @pl.when(c == 0)
            def _():
                for ahead in range(min(depth, n_chunks)):
                    start_chunk(e, c + ahead)

            finish_chunk(e, c, tgt)

            @pl.when(c + depth < n_chunks)
            def _():
                start_chunk(e, c + depth)

            return carry

        lax.fori_loop(0, cn, it, 0)

    @pl.when(j == 0)
    def _():
        y2[1] = jnp.zeros(y2.shape[1:], y2.dtype)
        dump = pltpu.make_async_copy(y2.at[1], o_hbm.at[pl.ds(o_hbm.shape[0] - 2 * blk, blk)],
                                     ssem.at[0])
        dump.start()
        dump.wait()
        stage(bexp_ref[0], 0, n_chunks, 0)
        _gather_rows(h_hbm, g0_ref, x3.at[0], gsem.at[0])
        _gather_rows(h_hbm, g1_ref, x3.at[1], gsem.at[1])

    @pl.when(j < nu)
    def _():
        xs = j % 3
        ys = j % 2
        ws = wslot_ref[j]
        nxt = (j + 2) % 3
        nq = EXPERT_F_CHUNKS
        fq, rq = f // nq, blk // nq
        _wait_rows(h_hbm, x3.at[xs], gsem.at[xs])

        x = x3[xs].astype(BF16)
        for q in range(nq):
            for r in range(q * rq, (q + 1) * rq):
                pltpu.make_async_copy(y2.at[1 - ys, pl.ds(r, 1)],
                                      o_hbm.at[pl.ds(s_ref[0, 0, r], 1)], ssem.at[1 - ys]).start()
                pltpu.make_async_copy(h_hbm.at[pl.ds(g2_ref[0, 0, r], 1)],
                                      x3.at[nxt, pl.ds(r, 1)], gsem.at[nxt]).start()
            fs = slice(q * fq, (q + 1) * fq)
            a = jnp.dot(x, wb1[ws, :, fs], preferred_element_type=F32)
            b = jnp.dot(x, wb3[ws, :, fs], preferred_element_type=F32)
            hm_scr[:, fs] = ((a * jax.nn.sigmoid(a)) * b).astype(BF16)

        @pl.when(j >= 1)
        def _():
            _wait_rows(o_hbm, y2.at[ys], ssem.at[ys])

        y2[ys] = jnp.dot(hm_scr[...], wb2[ws], preferred_element_type=F32)

        stage(se_ref[j], sc0_ref[j], scn_ref[j], 1 - ws)

        @pl.when(j == nu - 1)
        def _():
            _scatter_rows(y2.at[ys], sl_ref, o_hbm, ssem.at[ys])
            _wait_rows(o_hbm, y2.at[ys], ssem.at[ys])
            _wait_rows(o_hbm, y2.at[1 - ys], ssem.at[1 - ys])
            _wait_rows(h_hbm, x3.at[nxt], gsem.at[nxt])
            _wait_rows(h_hbm, x3.at[(j + 1) % 3], gsem.at[(j + 1) % 3])


def _expert_plan(block_exp, counts_padded, n_used, n_chunks, blk):
    ne = counts_padded.shape[0]
    nb = block_exp.shape[0]
    nblk = counts_padded // blk
    first = (jnp.cumsum(counts_padded) - counts_padded) // blk
    nonempty = nblk > 0
    ordinal = jnp.cumsum(nonempty.astype(jnp.int32)) - nonempty.astype(jnp.int32)
    cand = jnp.where(nonempty, jnp.arange(ne, dtype=jnp.int32), ne)
    suffix_min = jnp.flip(lax.cummin(jnp.flip(cand)))
    nxt = jnp.concatenate([suffix_min[1:], jnp.full((1,), ne, jnp.int32)])
    jj = jnp.arange(nb, dtype=jnp.int32)
    onehot = block_exp[:, None] == jnp.arange(ne, dtype=jnp.int32)[None, :]
    pick = lambda tab: jnp.sum(jnp.where(onehot, tab.astype(jnp.int32)[None, :], 0), axis=1)
    nxt_b = pick(nxt)
    has_b = nxt_b < ne
    m = jj - pick(first)
    n_e = jnp.maximum(pick(nblk), 1)
    quota = (n_chunks + n_e - 1) // n_e
    c0 = jnp.clip(m * quota, 0, n_chunks)
    cn = jnp.clip(n_chunks - c0, 0, quota)
    cn = jnp.where(has_b & (jj < n_used[0]), cn, 0)
    stage_e = jnp.where(has_b, nxt_b, 0)
    wslot = pick(ordinal) % 2
    i32 = lambda v: v.astype(jnp.int32)
    return i32(wslot), i32(stage_e), i32(c0), i32(cn)


def _experts(h2, row_tok, row_dst, block_exp, counts_padded, n_used, w1, w3, w2):
    t, d = h2.shape
    nb = block_exp.shape[0]
    blk = MOE_BLOCK
    f = w1.shape[2]
    rows_a = min(d, max(8, WEIGHT_CHUNK_ELEMS // f))
    rows_b = min(f, max(8, WEIGHT_CHUNK_ELEMS // d))
    n_chunks = 2 * (d // rows_a) + f // rows_b
    wslot, stage_e, c0, cn = _expert_plan(block_exp, counts_padded, n_used, n_chunks, blk)
    gidx = row_tok.reshape(nb, 1, blk)
    dump1 = 2 * t + blk + jnp.arange(blk, dtype=jnp.int32)
    sidx = jnp.concatenate([dump1, row_dst]).reshape(nb + 1, 1, blk)
    last = nb - 1
    smem = lambda m: pl.BlockSpec((1, 1, blk), m, memory_space=pltpu.SMEM)
    anyspec = pl.BlockSpec(memory_space=pl.ANY)
    grid_spec = pltpu.PrefetchScalarGridSpec(
        num_scalar_prefetch=6,
        grid=(nb,),
        in_specs=[smem(lambda j, *_: (0, 0, 0)),
                  smem(lambda j, *_: (min(1, last), 0, 0)),
                  smem(lambda j, *_: (jnp.minimum(j + 2, last), 0, 0)),
                  smem(lambda j, *_: (j, 0, 0)),
                  smem(lambda j, *_: (j + 1, 0, 0)),
                  anyspec, anyspec, anyspec, anyspec],
        out_specs=anyspec,
        scratch_shapes=[pltpu.VMEM((3, blk, d), F32),
                        pltpu.VMEM((2, blk, d), F32),
                        pltpu.VMEM((blk, f), BF16),
                        pltpu.VMEM((2, d, f), BF16), pltpu.VMEM((2, d, f), BF16),
                        pltpu.VMEM((2, f, d), BF16),
                        pltpu.VMEM((WEIGHT_CHUNKS_IN_FLIGHT, rows_a, f), F32),
                        pltpu.VMEM((WEIGHT_CHUNKS_IN_FLIGHT, rows_b, d), F32),
                        pltpu.SemaphoreType.DMA((3,)), pltpu.SemaphoreType.DMA((2,)),
                        pltpu.SemaphoreType.DMA((2 * WEIGHT_CHUNKS_IN_FLIGHT,))],
    )
    return pl.pallas_call(
        _expert_kernel,
        grid_spec=grid_spec,
        out_shape=jax.ShapeDtypeStruct((2 * t + 2 * blk, d), F32),
        compiler_params=_params(("arbitrary",)),
        name="experts",
    )(block_exp, n_used, wslot, stage_e, c0, cn, gidx, gidx, gidx, sidx, sidx, h2, w1, w3, w2)


def _final_kernel(p0_ref, p1_ref, xm_ref, w_ref, g2_ref, lg_ref, lb_ref, o_ref, *, alpha, rc):
    g2 = g2_ref[0]
    lg = lg_ref[...]
    lb = lb_ref[...]

    for r in range(xm_ref.shape[0] // rc):
        rows = pl.ds(r * rc, rc)
        w = w_ref[rows, :]
        y2 = w[:, 0:1] * p0_ref[rows, :] + w[:, 1:2] * p1_ref[rows, :]
        o_ref[rows, :] = _ln_rows(alpha * xm_ref[rows, :] + g2 * y2) * lg + lb


def _final(packed, x_mid, wts, gate2, l2g, l2b, seq, alpha, tm=256):
    t, d = x_mid.shape
    nt = t // tm
    kern = functools.partial(_final_kernel, alpha=alpha, rc=32)
    full = lambda a: pl.BlockSpec(a.shape, lambda i: (0,) * a.ndim)
    return pl.pallas_call(
        kern,
        grid=(nt,),
        in_specs=[pl.BlockSpec((tm, d), lambda i: (i, 0)),
                  pl.BlockSpec((tm, d), lambda i: (i + nt, 0)),
                  pl.BlockSpec((tm, d), lambda i: (i, 0)),
                  pl.BlockSpec((tm, 2), lambda i: (i, 0)),
                  pl.BlockSpec((1, 1, d), lambda i: ((i * tm) // seq, 0, 0)),
                  full(l2g), full(l2b)],
        out_specs=pl.BlockSpec((tm, d), lambda i: (i, 0)),
        out_shape=jax.ShapeDtypeStruct((t, d), F32),
        compiler_params=_params(("arbitrary",)),
        name="final",
    )(packed, packed, x_mid, wts, gate2, l2g, l2b)


def kernel(x, c, ctx, c_ctx, w_ada, b_ada, w_in, w_f_proj, lam_re, lam_im, log_dt, b_re, b_im,
           c_re, c_im, d_skip, w_glu, b_glu, w_s_proj, w_out, ln1_g, ln1_b, w_group, b_group,
           w_expert, b_expert, w1, w3, w2, ln2_g, ln2_b):
    depth = w_ada.shape[0]
    assert depth == 1, "single-layer block"
    bsz, seq, d = x.shape
    ctx_len = ctx.shape[1]
    t = bsz * seq
    fw = w_f_proj.shape[1]
    sw = w_s_proj.shape[1]
    gd = fw // F_GROUPS
    rows = seq // GRID_W
    alpha = (2.0 * depth) ** 0.25
    row2 = lambda v: v.reshape(1, -1).astype(F32)

    cc = jnp.concatenate([c, c_ctx[None], jnp.zeros((8 - bsz - 1, d), F32)], axis=0)
    mod = _adaln(cc, w_ada[0], b_ada[0].reshape(1, -1))
    mods = [mod[:, k * d:(k + 1) * d] for k in range(N_MOD)]
    per_b = lambda m: m[:bsz].reshape(bsz, 1, d)
    ctx_row = lambda m: m[bsz:bsz + 1].reshape(1, 1, d)

    wi = w_in[0].astype(BF16)
    tn = sw
    n_sig = (2 * d) // tn
    main, us = _ln_mod_matmul(x.reshape(t, d), per_b(mods[0]), per_b(mods[1]), wi,
                              rows_per_mod=seq, n_sig=n_sig, tm=1024, tn=tn,
                              nj=wi.shape[1] // tn, col_shift=(fw + sw) // tn)
    _, us_ctx = _ln_mod_matmul(ctx.reshape(bsz * ctx_len, d), ctx_row(mods[0]), ctx_row(mods[1]),
                               wi, rows_per_mod=bsz * ctx_len, n_sig=0, tm=ctx_len, tn=tn,
                               nj=1, col_shift=fw // tn)

    f_out = _fourier_mix(main, 2 * d, bsz, rows, GRID_W, gd)

    perm = _octet_perm()
    tables = _s5_tables(lam_re[0], lam_im[0], log_dt[0], b_re[0], b_im[0], c_re[0], c_im[0],
                        d_skip[0])
    zl = _relayout_in(us, perm)
    zc = _relayout_in(us_ctx, perm)
    y_chunks = _s5_core(zl, zc, tables, bsz)
    y_s = _relayout_out(y_chunks, perm)

    wr = jnp.concatenate([w_group[0], w_expert[0],
                          jnp.zeros((d, ROUTE_LANES - N_EXPERT_GROUPS - N_EXPERTS), F32)], axis=1)
    br = jnp.concatenate([b_group[0], b_expert[0],
                          jnp.zeros((ROUTE_LANES - N_EXPERT_GROUPS - N_EXPERTS,), F32)])
    x_mid, h2, logits = _merge(
        f_out, y_s, main, x.reshape(t, d), per_b(mods[2]), per_b(mods[3]), per_b(mods[4]),
        w_glu[0].astype(BF16), row2(b_glu[0]), w_f_proj[0].astype(BF16), w_s_proj[0].astype(BF16),
        w_out[0].astype(BF16), row2(ln1_g[0]), row2(ln1_b[0]), wr.astype(BF16), row2(br),
        seq, alpha)

    idx_t, wts, cnt = _route(logits)
    experts = idx_t[0:2].astype(jnp.int32)
    rank = idx_t[2:4].astype(jnp.int32)
    counts = cnt[0, :N_EXPERTS].astype(jnp.int32)
    blk = MOE_BLOCK
    n_assign = 2 * t
    nb = -(-(n_assign + N_EXPERTS * (blk - 1)) // blk)
    padded = (counts + blk - 1) // blk * blk
    pend = jnp.cumsum(padded)
    pstart = pend - padded
    eids = jnp.arange(N_EXPERTS, dtype=jnp.int32)[:, None, None]
    dest = jnp.sum(jnp.where(experts[None] == eids, pstart[:, None, None], 0), axis=0) + rank
    n_used = (pend[-1] // blk).astype(jnp.int32).reshape(1)
    block_row = jnp.arange(nb, dtype=jnp.int32) * blk
    block_exp = jnp.minimum(jnp.sum((pend[None, :] <= block_row[:, None]).astype(jnp.int32), axis=1),
                            N_EXPERTS - 1)
    assign = jnp.zeros((nb * blk,), jnp.int32).at[dest.reshape(-1)].add(
        jnp.arange(1, n_assign + 1, dtype=jnp.int32)) - 1
    valid = assign >= 0
    row_tok = jnp.where(valid, assign % t, 0)
    row_in_2blk = jnp.arange(nb * blk, dtype=jnp.int32) % (2 * blk)
    row_dst = jnp.where(valid, assign, 2 * t + row_in_2blk)

    packed = _experts(h2, row_tok, row_dst, block_exp, padded, n_used, w1[0], w3[0], w2[0])
    out = _final(packed, x_mid, wts, per_b(mods[5]), row2(ln2_g[0]), row2(ln2_b[0]), seq, alpha)
    return out.reshape(bsz, seq, d)
```

```python
import functools
import math

import numpy as np
import jax
import jax.numpy as jnp
from jax import lax
from jax.experimental import pallas as pl
from jax.experimental.pallas import tpu as pltpu

F32 = jnp.float32
BF16 = jnp.bfloat16

GRID_W = 64
F_GROUPS = 4
S_GROUP_DIM = 16
S_STATE = 64
N_EXPERT_GROUPS = 4
EXPERTS_PER_GROUP = 8
N_EXPERTS = N_EXPERT_GROUPS * EXPERTS_PER_GROUP
N_MOD = 6
LN_EPS = 1e-5

LANES = 128
S5_CHUNK = 64
OCTET = LANES // S_GROUP_DIM
FOURIER_COL_GROUP = 8
MOE_BLOCK = 256
WEIGHT_CHUNK_ELEMS = 256 * 1024
WEIGHT_CHUNKS_IN_FLIGHT = 4
EXPERT_F_CHUNKS = 4
ROUTE_LANES = 128
VMEM_LIMIT = 56 * 1024 * 1024


def _params(sem, vmem=VMEM_LIMIT):
    return pltpu.CompilerParams(dimension_semantics=sem, vmem_limit_bytes=vmem)


def _ln_rows(x):
    mu = jnp.mean(x, axis=-1, keepdims=True)
    xc = x - mu
    var = jnp.mean(xc * xc, axis=-1, keepdims=True)
    return xc * lax.rsqrt(var + LN_EPS)


def _adaln_kernel(c_ref, w_ref, b_ref, o_ref):
    s = c_ref[...]
    s = s * jax.nn.sigmoid(s)
    o_ref[...] = jnp.dot(s.astype(BF16), w_ref[...].astype(BF16),
                         preferred_element_type=F32) + b_ref[...]


def _adaln(cc, w, b, tn=1024):
    m, d = cc.shape
    n = w.shape[1]
    return pl.pallas_call(
        _adaln_kernel,
        grid=(n // tn,),
        in_specs=[pl.BlockSpec((m, d), lambda j: (0, 0)),
                  pl.BlockSpec((d, tn), lambda j: (0, j)),
                  pl.BlockSpec((1, tn), lambda j: (0, j))],
        out_specs=pl.BlockSpec((m, tn), lambda j: (0, j)),
        out_shape=jax.ShapeDtypeStruct((m, n), F32),
        compiler_params=_params(("arbitrary",)),
        name="adaln",
    )(cc, w, b)


def _ln_mod_rows(x_ref, rows, sc1, sh):
    mu = jnp.mean(x_ref[rows, :], axis=-1, keepdims=True)
    xc = x_ref[rows, :] - mu
    var = jnp.mean(xc * xc, axis=-1, keepdims=True)
    return (x_ref[rows, :] - mu) * lax.rsqrt(var + LN_EPS) * sc1 + sh


def _lnmm_kernel(x0_ref, sh0_ref, sc0_ref, xn_ref, shn_ref, scn_ref, w_ref, o_ref, us_ref,
                 ha, hb, *, n_sig, nrc, mchunks):
    i = pl.program_id(0)
    j = pl.program_id(1)
    tm = xn_ref.shape[0]
    rcs = tm // nrc

    @pl.when((i == 0) & (j == 0))
    def _():
        def body(r, carry):
            rows = pl.ds(pl.multiple_of(r * rcs, rcs), rcs)
            ha[rows, :] = _ln_mod_rows(x0_ref, rows, 1.0 + sc0_ref[0], sh0_ref[0]).astype(BF16)
            return carry

        lax.fori_loop(0, nrc, body, 0)

    def step(h_cur, h_nxt):
        tmc = tm // mchunks
        for m in range(mchunks):
            rs = pl.ds(m * tmc, tmc)
            acc = jnp.dot(h_cur[rs, :], w_ref[...], preferred_element_type=F32)
            o_ref[rs, :] = jnp.where(j < n_sig, jax.nn.sigmoid(acc), acc).astype(o_ref.dtype)
            us_ref[rs, :] = acc
        c = jnp.minimum(j, nrc - 1)
        rows = pl.ds(pl.multiple_of(c * rcs, rcs), rcs)
        h_nxt[rows, :] = _ln_mod_rows(xn_ref, rows, 1.0 + scn_ref[0], shn_ref[0]).astype(BF16)

    @pl.when(i % 2 == 0)
    def _():
        step(ha, hb)

    @pl.when(i % 2 == 1)
    def _():
        step(hb, ha)


def _ln_mod_matmul(x2, shift, scale, w, rows_per_mod, n_sig, tm, tn, nj, col_shift):
    t, d = x2.shape
    nw = w.shape[1] // tn
    n = nj * tn
    nt = t // tm
    nrc = min(8, nj)
    kern = functools.partial(_lnmm_kernel, n_sig=n_sig, nrc=nrc, mchunks=max(1, tm // 256))
    nxt = lambda i: jnp.minimum(i + 1, nt - 1)
    once = pl.Buffered(1)
    mod0 = pl.BlockSpec((1, 1, d), lambda i, j: (0, 0, 0), pipeline_mode=once)
    modn = pl.BlockSpec((1, 1, d), lambda i, j: ((nxt(i) * tm) // rows_per_mod, 0, 0))
    return pl.pallas_call(
        kern,
        grid=(nt, nj),
        in_specs=[pl.BlockSpec((tm, d), lambda i, j: (0, 0), pipeline_mode=once), mod0, mod0,
                  pl.BlockSpec((tm, d), lambda i, j: (nxt(i), 0)), modn, modn,
                  pl.BlockSpec((d, tn), lambda i, j: (0, (j + col_shift) % nw))],
        out_specs=[pl.BlockSpec((tm, tn), lambda i, j: (i, j)),
                   pl.BlockSpec((tm, tn), lambda i, j: (i, 0))],
        out_shape=[jax.ShapeDtypeStruct((t, n), BF16),
                   jax.ShapeDtypeStruct((t, tn), F32)],
        scratch_shapes=[pltpu.VMEM((tm, d), BF16), pltpu.VMEM((tm, d), BF16)],
        compiler_params=_params(("arbitrary", "arbitrary")),
        name="ln_mod_matmul",
    )(x2, shift, scale, x2, shift, scale, w)


def _dft_cs(n):
    k = np.arange(n)
    ang = 2.0 * np.pi * np.outer(k, k) / n
    return np.cos(ang), np.sin(ang)


def _fourier_tables(rows, cols, gd):
    c1, s1 = _dft_cs(gd)
    half = gd // 2
    w1 = np.stack([np.concatenate([c1[:, h * half:(h + 1) * half],
                                   -s1[:, h * half:(h + 1) * half]], axis=1)
                   for h in range(2)]) / math.sqrt(gd)
    c2, s2 = _dft_cs(cols)
    rep = 256 // cols
    eye = np.eye(rep)
    l2 = np.concatenate([np.kron(eye, c2), np.kron(eye, s2)], axis=0) / math.sqrt(cols)
    c3, s3 = _dft_cs(rows)
    m3 = np.concatenate([c3, s3], axis=0)
    return tuple(jnp.asarray(a, F32).astype(BF16) for a in (w1, l2, m3))


def _fourier_kernel(x_ref, w1_ref, l2_ref, m3_ref, o_ref, a_scr, b_scr, *, rows, cols, scale3):
    npos = rows * cols
    half = o_ref.shape[2]
    slab = l2_ref.shape[1]
    rps = slab // cols
    rc = 1024 if npos % 1024 == 0 else npos

    def s1(r, carry):
        rs = pl.ds(pl.multiple_of(r * rc, rc), rc)
        a_scr[rs, :] = jnp.dot(x_ref[rs, :], w1_ref[0],
                               preferred_element_type=F32).astype(BF16)
        return carry

    lax.fori_loop(0, npos // rc, s1, 0)

    unroll = 4 if (npos // slab) % 4 == 0 else 1

    def s2(s, carry):
        for k in range(unroll):
            si = s * unroll + k
            rs = pl.ds(pl.multiple_of(si * slab, slab), slab)
            g = jnp.dot(l2_ref[...], a_scr[rs, :], preferred_element_type=F32)
            rr = pl.ds(si * rps, rps)
            b_scr[0, rr] = (g[0:slab, 0:half] + g[slab:, half:]).reshape(rps, cols, half)
            b_scr[1, rr] = (g[0:slab, half:] - g[slab:, 0:half]).reshape(rps, cols, half)
        return carry

    lax.fori_loop(0, npos // slab // unroll, s2, 0)

    cg = FOURIER_COL_GROUP
    for c0 in range(0, cols, cg):
        xr = jnp.swapaxes(b_scr[0, :, c0:c0 + cg, :], 0, 1)
        xi = jnp.swapaxes(b_scr[1, :, c0:c0 + cg, :], 0, 1)
        bc = jnp.concatenate([p for k in range(cg) for p in (xr[k], xi[k])],
                             axis=1).astype(BF16)
        h = jnp.dot(m3_ref[...], bc, preferred_element_type=F32)
        ys = [h[0:rows, 2 * k * half:(2 * k + 1) * half]
              + h[rows:, (2 * k + 1) * half:(2 * k + 2) * half] for k in range(cg)]
        o_ref[:, c0:c0 + cg, :] = jnp.swapaxes(jnp.stack(ys, axis=0), 0, 1) * scale3


def _fourier_mix(u_all, col0, batch, rows, cols, gd):
    npos = rows * cols
    half = gd // 2
    w1, l2, m3 = _fourier_tables(rows, cols, gd)
    kern = functools.partial(_fourier_kernel, rows=rows, cols=cols, scale3=1.0 / math.sqrt(rows))
    cb0 = col0 // gd
    return pl.pallas_call(
        kern,
        grid=(batch, F_GROUPS, 2),
        in_specs=[pl.BlockSpec((npos, gd), lambda b, g, h: (b, cb0 + g)),
                  pl.BlockSpec((1, gd, gd), lambda b, g, h: (h, 0, 0)),
                  pl.BlockSpec(l2.shape, lambda b, g, h: (0, 0)),
                  pl.BlockSpec(m3.shape, lambda b, g, h: (0, 0))],
        out_specs=pl.BlockSpec((rows, cols, half), lambda b, g, h: (b, 0, g * 2 + h)),
        out_shape=jax.ShapeDtypeStruct((batch * rows, cols, F_GROUPS * gd), F32),
        scratch_shapes=[pltpu.VMEM((npos, gd), BF16), pltpu.VMEM((2, rows, cols, half), F32)],
        compiler_params=_params(("arbitrary", "arbitrary", "arbitrary")),
        name="fourier_mix",
    )(u_all, w1, l2, m3).reshape(batch * npos, F_GROUPS * gd)


def _octet_perm():
    n = OCTET * LANES
    p = np.zeros((n, n), np.float32)
    for k in range(OCTET):
        for g in range(OCTET):
            for h in range(S_GROUP_DIM):
                p[k * LANES + g * S_GROUP_DIM + h, g * LANES + k * S_GROUP_DIM + h] = 1.0
    return jnp.asarray(p, BF16)


def _relayout_in_kernel(x_ref, p_ref, o_ref):
    io = pl.program_id(1)
    tiles = x_ref[:, pl.ds(pl.multiple_of(io * OCTET, OCTET), OCTET), :]
    by_pos = jnp.swapaxes(tiles, 0, 1)
    a = jnp.concatenate([by_pos[k] for k in range(OCTET)], axis=1).astype(BF16)
    b = jnp.dot(a, p_ref[...], preferred_element_type=F32).astype(BF16)
    for g in range(OCTET):
        o_ref[g] = b[:, g * LANES:(g + 1) * LANES]


def _relayout_in(us, perm):
    t, sw = us.shape
    nchunk = t // S5_CHUNK
    ngroups = sw // S_GROUP_DIM
    return pl.pallas_call(
        _relayout_in_kernel,
        grid=(sw // LANES, S5_CHUNK // OCTET),
        in_specs=[pl.BlockSpec((nchunk, S5_CHUNK, LANES), lambda o, io: (0, 0, o)),
                  pl.BlockSpec(perm.shape, lambda o, io: (0, 0))],
        out_specs=pl.BlockSpec((OCTET, nchunk, LANES), lambda o, io: (o, 0, io)),
        out_shape=jax.ShapeDtypeStruct((ngroups, nchunk, S5_CHUNK * S_GROUP_DIM), BF16),
        compiler_params=_params(("arbitrary", "arbitrary")),
        name="s5_relayout_in",
    )(us.reshape(nchunk, S5_CHUNK, sw), perm)


def _relayout_out_kernel(y_ref, p_ref, o_ref):
    io = pl.program_id(1)
    b = jnp.concatenate([y_ref[g] for g in range(OCTET)], axis=1)
    a = jnp.dot(b, p_ref[...], preferred_element_type=F32)
    by_pos = jnp.stack([a[:, k * LANES:(k + 1) * LANES] for k in range(OCTET)], axis=0)
    o_ref[:, pl.ds(pl.multiple_of(io * OCTET, OCTET), OCTET), :] = jnp.swapaxes(by_pos, 0, 1)


def _relayout_out(y, perm):
    ngroups, nchunk, _ = y.shape
    t = nchunk * S5_CHUNK
    sw = ngroups * S_GROUP_DIM
    return pl.pallas_call(
        _relayout_out_kernel,
        grid=(sw // LANES, S5_CHUNK // OCTET),
        in_specs=[pl.BlockSpec((OCTET, nchunk, LANES), lambda o, io: (o, 0, io)),
                  pl.BlockSpec(perm.shape, lambda o, io: (0, 0))],
        out_specs=pl.BlockSpec((nchunk, S5_CHUNK, LANES), lambda o, io: (0, 0, o)),
        out_shape=jax.ShapeDtypeStruct((nchunk, S5_CHUNK, sw), F32),
        compiler_params=_params(("arbitrary", "arbitrary")),
        name="s5_relayout_out",
    )(y, perm).reshape(t, sw)


def _s5_tables(lam_re, lam_im, log_dt, b_re, b_im, c_re, c_im, d_skip):
    lc = S5_CHUNK

    e_all = jnp.arange(-(lc - 1), lc + 1, dtype=F32)

    def direction(d):
        lr, li = lam_re[d].astype(F32), lam_im[d].astype(F32)
        dt = jnp.exp(log_dt[d].astype(F32))[:, None]
        zr, zi = lr * dt, li * dt
        mag = jnp.exp(zr)
        nr, ni = mag * jnp.cos(zi) - 1.0, mag * jnp.sin(zi)
        den = lr * lr + li * li
        qr, qi = (nr * lr + ni * li) / den, (ni * lr - nr * li) / den
        br, bi = b_re[d].astype(F32), b_im[d].astype(F32)
        bbr = qr[:, :, None] * br - qi[:, :, None] * bi
        bbi = qr[:, :, None] * bi + qi[:, :, None] * br
        bbar = (jnp.swapaxes(bbr, 1, 2), jnp.swapaxes(bbi, 1, 2))
        cc = (c_re[d].astype(F32), c_im[d].astype(F32))
        pm = jnp.exp(zr[:, None, :] * e_all[None, :, None])
        ang = zi[:, None, :] * e_all[None, :, None]
        return (pm * jnp.cos(ang), pm * jnp.sin(ang)), bbar, cc

    powf, bf, cf = direction(0)
    powb, bb, cb = direction(1)

    def up(t, lo):
        return (t[0][:, lo + lc - 1:lo + 2 * lc - 1], t[1][:, lo + lc - 1:lo + 2 * lc - 1])

    def down(t, hi):
        return (jnp.flip(t[0][:, hi:hi + lc], axis=1), jnp.flip(t[1][:, hi:hi + lc], axis=1))

    cat = lambda x, y: jnp.concatenate([x, y], axis=-1)
    dup = lambda t: [cat(t[0], t[0]), cat(t[1], t[1])]
    f_d0, f_u0, b_u0, b_d0 = down(powf, 0), up(powf, 0), up(powb, 0), down(powb, 0)
    f_dl, f_u1, b_dl = down(powf, lc - 1), up(powf, 1), down(powb, lc)
    pow_fac = jnp.stack(dup(f_d0) + dup(b_u0) + dup(f_u0) + dup(b_d0)
                        + [cat(f_dl[0], b_u0[0]), cat(f_dl[1], b_u0[1])]
                        + [cat(f_u1[0], b_dl[0]), cat(f_u1[1], b_dl[1])], axis=0)
    (bfr, bfi), (bbr, bbi) = bf, bb
    (cfr, cfi), (cbr, cbi) = cf, cb
    vec_fac = jnp.stack([cat(bfr, -bfi), cat(-bfi, -bfr),
                         cat(bbr, -bbi), cat(-bbi, -bbr),
                         cat(cfr, cfi), cat(-cfi, cfr),
                         cat(cbr, cbi), cat(-cbi, cbr),
                         cat(bfr, bbr), cat(-bfi, -bbi),
                         cat(bfi, bbi), cat(bfr, bbr),
                         cat(cfr, cbr), cat(-cfi, -cbi),
                         cat(-cfi, -cbi), cat(-cfr, -cbr)], axis=0)
    last = 2 * lc - 1
    decay = jnp.stack([jnp.concatenate([powf[0][:, last], powb[0][:, last]], axis=1),
                       jnp.concatenate([powf[1][:, last], powb[1][:, last]], axis=1)],
                      axis=1)
    dvec = jnp.tile(d_skip.astype(F32), (1, lc))[:, None, :]
    return pow_fac, vec_fac, decay, dvec


_NT = (((1,), (1,)), ((), ()))


def _s5_kernel(z_ref, zc_ref, pf_ref, vf_ref, dec_ref, dv_ref,
               y_ref, e_scr, ec_scr, srf_scr, srb_scr, sif_scr, sib_scr, *, batch, nlat, nctx):
    p2 = 2 * S_STATE
    z = z_ref[0]

    def table(ip, iv):
        pa, pb = pf_ref[ip, 0], pf_ref[ip + 1, 0]
        va, vb = vf_ref[iv, 0], vf_ref[iv + 1, 0]
        t = pa[:, None, :] * va[None, :, :] + pb[:, None, :] * vb[None, :, :]
        return t.reshape(pa.shape[0] * va.shape[0], pa.shape[1])

    pmat = jnp.concatenate([table(8, 8), table(8, 10)], axis=1).astype(BF16)
    e = jnp.dot(z, pmat, preferred_element_type=F32)
    ec = jnp.dot(zc_ref[0], pmat, preferred_element_type=F32)
    for col in range(2):
        e_scr[col] = e[:, col * p2:(col + 1) * p2]
        ec_scr[col] = ec[:, col * p2:(col + 1) * p2]

    ar = dec_ref[0, 0:1, :]
    ai = dec_ref[0, 1:2, :]
    lane = lax.broadcasted_iota(jnp.int32, (batch, p2), 1)
    is_f = lane < S_STATE

    def advance(sr, si, er, ei):
        return ar * sr - ai * si + er, ar * si + ai * sr + ei

    def pick(ref, nf, nb, n, col):
        f = ref[col, pl.ds(nf, batch, stride=n), :]
        b = ref[col, pl.ds(nb, batch, stride=n), :]
        return jnp.where(is_f, f, b)

    sr = jnp.zeros((batch, p2), F32)
    si = jnp.zeros((batch, p2), F32)
    for k in range(nctx):
        nf, nb = k, nctx - 1 - k
        sr, si = advance(sr, si, pick(ec_scr, nf, nb, nctx, 0), pick(ec_scr, nf, nb, nctx, 1))
    for k in range(nlat):
        nf, nb = k, nlat - 1 - k
        rf = pl.ds(nf, batch, stride=nlat)
        rb = pl.ds(nb, batch, stride=nlat)
        srf_scr[rf, :] = sr
        srb_scr[rb, :] = sr
        sif_scr[rf, :] = si
        sib_scr[rb, :] = si
        sr, si = advance(sr, si, pick(e_scr, nf, nb, nlat, 0), pick(e_scr, nf, nb, nlat, 1))
    lane_all = lax.broadcasted_iota(jnp.int32, srf_scr.shape, 1)
    s_re = jnp.where(lane_all < S_STATE, srf_scr[...], srb_scr[...])
    s_im = jnp.where(lane_all < S_STATE, sif_scr[...], sib_scr[...])
    s_all = jnp.concatenate([s_re, s_im], axis=1).astype(BF16)

    n = z.shape[1]
    tf = lax.dot_general(table(0, 0).astype(BF16), table(4, 4).astype(BF16), _NT,
                         preferred_element_type=F32)
    tb = lax.dot_general(table(2, 2).astype(BF16), table(6, 6).astype(BF16), _NT,
                         preferred_element_type=F32)
    ri = lax.broadcasted_iota(jnp.int32, (n, n), 0)
    ci = lax.broadcasted_iota(jnp.int32, (n, n), 1)
    rpos = ri // S_GROUP_DIM
    cpos = ci // S_GROUP_DIM
    tmat = (jnp.where(rpos <= cpos, tf, 0.0) + jnp.where(rpos >= cpos, tb, 0.0)
            + jnp.where(ri == ci, dv_ref[0], 0.0))
    y = jnp.dot(z, tmat.astype(BF16), preferred_element_type=F32)
    qmat_t = jnp.concatenate([table(10, 12), table(10, 14)], axis=1).astype(BF16)
    y = y + lax.dot_general(s_all, qmat_t, _NT, preferred_element_type=F32)
    y_ref[0] = y.astype(y_ref.dtype)


def _s5_core(zl, zc, tables, batch):
    pow_fac, vec_fac, decay, dvec = tables
    g, rl, n = zl.shape
    rcx = zc.shape[1]
    nlat, nctx = rl // batch, rcx // batch
    kern = functools.partial(_s5_kernel, batch=batch, nlat=nlat, nctx=nctx)
    gmap = lambda i: (i, 0, 0)
    spec = lambda a: pl.BlockSpec((1,) + a.shape[1:], lambda i: (i,) + (0,) * (a.ndim - 1))
    fac_spec = lambda a: pl.BlockSpec((a.shape[0], 1) + a.shape[2:], lambda i: (0, i, 0, 0))
    return pl.pallas_call(
        kern,
        grid=(g,),
        in_specs=[spec(zl), spec(zc), fac_spec(pow_fac), fac_spec(vec_fac), spec(decay), spec(dvec)],
        out_specs=pl.BlockSpec((1, rl, n), gmap),
        out_shape=jax.ShapeDtypeStruct((g, rl, n), BF16),
        scratch_shapes=[pltpu.VMEM((2, rl, 2 * S_STATE), F32),
                        pltpu.VMEM((2, rcx, 2 * S_STATE), F32)]
                       + [pltpu.VMEM((rl, 2 * S_STATE), F32)] * 4,
        compiler_params=_params(("arbitrary",)),
        name="s5_core",
    )(zl, zc, pow_fac, vec_fac, decay, dvec)


def _merge_kernel(fo_ref, ys_ref, zf_ref, zs_ref, x_ref, g1_ref, sh2_ref, sc2_ref,
                  wglu_ref, bglu_ref, wf_ref, ws_ref, wo_ref, l1g_ref, l1b_ref, wr_ref, br_ref,
                  xm_ref, h2_ref, lg_ref, mix_a, mix_b, h2_scr, *, alpha, rc):
    i = pl.program_id(0)

    @pl.when(i == 0)
    def _():
        mix_b[...] = jnp.zeros_like(mix_b)

    def step(mix_cur, mix_prv):
        y = jax.nn.gelu(ys_ref[...], approximate=True)
        glu = jnp.dot(y.astype(BF16), wglu_ref[...], preferred_element_type=F32) + bglu_ref[...]
        s_out = y * jax.nn.sigmoid(glu)
        a = jnp.dot(fo_ref[...].astype(BF16), wf_ref[...], preferred_element_type=F32)
        b = jnp.dot(s_out.astype(BF16), ws_ref[...], preferred_element_type=F32)
        m = zf_ref[...].astype(F32) * a + zs_ref[...].astype(F32) * b
        mix_cur[...] = jnp.dot(m.astype(BF16), wo_ref[...], preferred_element_type=F32)

        g1 = g1_ref[0]
        sh2 = sh2_ref[0]
        sc2 = 1.0 + sc2_ref[0]
        l1g = l1g_ref[...]
        l1b = l1b_ref[...]
        for r in range(x_ref.shape[0] // rc):
            rows = pl.ds(r * rc, rc)
            xm = _ln_rows(alpha * x_ref[rows, :] + g1 * mix_prv[rows, :]) * l1g + l1b
            xm_ref[rows, :] = xm
            h2 = _ln_rows(xm) * sc2 + sh2
            h2_ref[rows, :] = h2
            h2_scr[rows, :] = h2.astype(BF16)
        lg_ref[...] = jnp.dot(h2_scr[...], wr_ref[...], preferred_element_type=F32) + br_ref[...]

    @pl.when(i % 2 == 0)
    def _():
        step(mix_a, mix_b)

    @pl.when(i % 2 == 1)
    def _():
        step(mix_b, mix_a)


def _merge(f_out, y_s, main, x2, gate1, shift2, scale2, wglu, bglu, wf, ws, wo, l1g, l1b, wr, br,
           seq, alpha, tm=256):
    t, d = x2.shape
    nt = t // tm
    fw, sw = f_out.shape[1], y_s.shape[1]
    kern = functools.partial(_merge_kernel, alpha=alpha, rc=32)
    cur = lambda i: jnp.minimum(i, nt - 1)
    prv = lambda i: jnp.maximum(i - 1, 0)
    rowc = lambda w: pl.BlockSpec((tm, w), lambda i: (cur(i), 0))
    rowp = lambda w: pl.BlockSpec((tm, w), lambda i: (prv(i), 0))
    mod = pl.BlockSpec((1, 1, d), lambda i: ((prv(i) * tm) // seq, 0, 0))
    full = lambda a: pl.BlockSpec(a.shape, lambda i: (0,) * a.ndim, pipeline_mode=pl.Buffered(1))
    return pl.pallas_call(
        kern,
        grid=(nt + 1,),
        in_specs=[rowc(fw), rowc(sw),
                  pl.BlockSpec((tm, d), lambda i: (cur(i), 0)),
                  pl.BlockSpec((tm, d), lambda i: (cur(i), 1)),
                  rowp(d), mod, mod, mod,
                  full(wglu), full(bglu), full(wf), full(ws), full(wo), full(l1g), full(l1b),
                  full(wr), full(br)],
        out_specs=[rowp(d), rowp(d), rowp(ROUTE_LANES)],
        out_shape=[jax.ShapeDtypeStruct((t, d), F32), jax.ShapeDtypeStruct((t, d), F32),
                   jax.ShapeDtypeStruct((t, ROUTE_LANES), F32)],
        scratch_shapes=[pltpu.VMEM((tm, d), F32), pltpu.VMEM((tm, d), F32),
                        pltpu.VMEM((tm, d), BF16)],
        compiler_params=_params(("arbitrary",)),
        name="merge",
    )(f_out, y_s, main, main, x2, gate1, shift2, scale2, wglu, bglu, wf, ws, wo, l1g, l1b, wr, br)


def _route_kernel(lg_ref, tri_ref, idx_ref, w_ref, cnt_ref, run_scr):
    i = pl.program_id(0)

    @pl.when(i == 0)
    def _():
        run_scr[...] = jnp.zeros_like(run_scr)

    lg = lg_ref[...]
    tm = lg.shape[0]
    lane = lax.broadcasted_iota(jnp.int32, lg.shape, 1)
    neg = jnp.float32(-1e30)
    big = jnp.int32(ROUTE_LANES)
    ng, epg = N_EXPERT_GROUPS, EXPERTS_PER_GROUP

    is_g = lane < ng
    gl = jnp.where(is_g, lg, neg)
    gm = jnp.max(gl, axis=-1, keepdims=True)
    gidx = jnp.min(jnp.where(gl == gm, lane, big), axis=-1, keepdims=True)
    gsum = jnp.sum(jnp.where(is_g, jnp.exp(gl - gm), 0.0), axis=-1, keepdims=True)
    g_top = 1.0 / gsum

    lo = ng + gidx * epg
    el = jnp.where((lane >= lo) & (lane < lo + epg), lg, neg)
    m1 = jnp.max(el, axis=-1, keepdims=True)
    i1 = jnp.min(jnp.where(el == m1, lane, big), axis=-1, keepdims=True)
    el2 = jnp.where(lane == i1, neg, el)
    m2 = jnp.max(el2, axis=-1, keepdims=True)
    i2 = jnp.min(jnp.where(el2 == m2, lane, big), axis=-1, keepdims=True)
    tt = jnp.exp(m2 - m1)
    w0 = g_top / (1.0 + tt)
    w1 = g_top * tt / (1.0 + tt)
    e0 = i1 - ng
    e1 = i2 - ng

    oh = (lane == e0) | (lane == e1)
    pre = jnp.dot(tri_ref[...], jnp.where(oh, 1.0, 0.0).astype(BF16),
                  preferred_element_type=F32) + run_scr[...]
    r0 = jnp.sum(jnp.where(lane == e0, pre, 0.0), axis=-1, keepdims=True)
    r1 = jnp.sum(jnp.where(lane == e1, pre, 0.0), axis=-1, keepdims=True)
    run_scr[...] = run_scr[...] + jnp.sum(jnp.where(oh, 1.0, 0.0), axis=0, keepdims=True)
    cnt_ref[...] = run_scr[...]

    vals = (e0.astype(F32), e1.astype(F32), r0, r1)
    info = jnp.zeros(lg.shape, F32)
    for k, v in enumerate(vals):
        info = jnp.where(lane == k, v, info)
    sel = (lax.broadcasted_iota(jnp.int32, (8, ROUTE_LANES), 0)
           == lax.broadcasted_iota(jnp.int32, (8, ROUTE_LANES), 1)).astype(BF16)
    hi = jnp.floor(info * (1.0 / 256.0))
    lo = info - 256.0 * hi
    t_hi = lax.dot_general(sel, hi.astype(BF16), _NT, preferred_element_type=F32)
    t_lo = lax.dot_general(sel, lo.astype(BF16), _NT, preferred_element_type=F32)
    idx_ref[...] = 256.0 * t_hi + t_lo
    lane2 = lax.broadcasted_iota(jnp.int32, (tm, 2), 1)
    w_ref[...] = jnp.where(lane2 == 0, w0, w1)


def _route(logits, tm=512):
    t = logits.shape[0]
    assert 2 * t <= 256 * 256, "ranks are carried through the transpose as two base-256 digits"
    tri = jnp.asarray(np.tril(np.ones((tm, tm), np.float32), -1), BF16)
    return pl.pallas_call(
        _route_kernel,
        grid=(t // tm,),
        in_specs=[pl.BlockSpec((tm, ROUTE_LANES), lambda i: (i, 0)),
                  pl.BlockSpec((tm, tm), lambda i: (0, 0))],
        out_specs=[pl.BlockSpec((8, tm), lambda i: (0, i)),
                   pl.BlockSpec((tm, 2), lambda i: (i, 0)),
                   pl.BlockSpec((1, ROUTE_LANES), lambda i: (0, 0))],
        out_shape=[jax.ShapeDtypeStruct((8, t), F32),
                   jax.ShapeDtypeStruct((t, 2), F32),
                   jax.ShapeDtypeStruct((1, ROUTE_LANES), F32)],
        scratch_shapes=[pltpu.VMEM((1, ROUTE_LANES), F32)],
        compiler_params=_params(("arbitrary",)),
        name="route",
    )(logits, tri)


def _gather_rows(src_hbm, idx_ref, dst, sem):
    for r in range(dst.shape[0]):
        pltpu.make_async_copy(src_hbm.at[pl.ds(idx_ref[0, 0, r], 1)],
                              dst.at[pl.ds(r, 1)], sem).start()


def _scatter_rows(src, idx_ref, dst_hbm, sem):
    for r in range(src.shape[0]):
        pltpu.make_async_copy(src.at[pl.ds(r, 1)],
                              dst_hbm.at[pl.ds(idx_ref[0, 0, r], 1)], sem).start()


def _wait_rows(any_hbm, buf, sem):
    for r in range(buf.shape[0]):
        pltpu.make_async_copy(any_hbm.at[pl.ds(0, 1)], buf.at[pl.ds(r, 1)], sem).wait()


def _expert_kernel(bexp_ref, nu_ref, wslot_ref, se_ref, sc0_ref, scn_ref,
                   g0_ref, g1_ref, g2_ref, s_ref, sl_ref, h_hbm, w1_hbm, w3_hbm, w2_hbm,
                   o_hbm, x3, y2, hm_scr, wb1, wb3, wb2, st_a, st_b, gsem, ssem, wsem):
    j = pl.program_id(0)
    nu = nu_ref[0]
    blk, d = x3.shape[1], x3.shape[2]
    f = wb1.shape[2]
    depth = st_a.shape[0]
    rows_a = st_a.shape[1]
    rows_b = st_b.shape[1]
    n_a, n_b = d // rows_a, f // rows_b
    n_chunks = 2 * n_a + n_b

    def chunk_copy(e, c, kind):
        slot = c % depth
        if kind == 0:
            return pltpu.make_async_copy(w1_hbm.at[e, pl.ds(c * rows_a, rows_a), :],
                                         st_a.at[slot], wsem.at[slot])
        if kind == 1:
            return pltpu.make_async_copy(w3_hbm.at[e, pl.ds((c - n_a) * rows_a, rows_a), :],
                                         st_a.at[slot], wsem.at[slot])
        return pltpu.make_async_copy(w2_hbm.at[e, pl.ds((c - 2 * n_a) * rows_b, rows_b), :],
                                     st_b.at[slot], wsem.at[depth + slot])

    def kind_is(c, kind):
        lo = (0, n_a, 2 * n_a)[kind]
        hi = (n_a, 2 * n_a, n_chunks)[kind]
        return (c >= lo) & (c < hi)

    def start_chunk(e, c):
        for kind in range(3):
            @pl.when(kind_is(c, kind))
            def _():
                chunk_copy(e, c, kind).start()

    def finish_chunk(e, c, tgt):
        slot = c % depth
        for kind in range(3):
            @pl.when(kind_is(c, kind))
            def _():
                chunk_copy(e, c, kind).wait()
                if kind == 0:
                    rs = pl.ds(pl.multiple_of(c * rows_a, rows_a), rows_a)
                    wb1[tgt, rs, :] = st_a[slot].astype(BF16)
                elif kind == 1:
                    rs = pl.ds(pl.multiple_of((c - n_a) * rows_a, rows_a), rows_a)
                    wb3[tgt, rs, :] = st_a[slot].astype(BF16)
                else:
                    rs = pl.ds(pl.multiple_of((c - 2 * n_a) * rows_b, rows_b), rows_b)
                    wb2[tgt, rs, :] = st_b[slot].astype(BF16)

    def stage(e, c0, cn, tgt):
        def it(k, carry):
            c = c0 + k

            @pl.when(c == 0)
            def _():
                for ahead in range(min(depth, n_chunks)):
                    start_chunk(e, c + ahead)

            finish_chunk(e, c, tgt)

            @pl.when(c + depth < n_chunks)
            def _():
                start_chunk(e, c + depth)

            return carry

        lax.fori_loop(0, cn, it, 0)

    @pl.when(j == 0)
    def _():
        y2[1] = jnp.zeros(y2.shape[1:], y2.dtype)
        dump = pltpu.make_async_copy(y2.at[1], o_hbm.at[pl.ds(o_hbm.shape[0] - 2 * blk, blk)],
                                     ssem.at[0])
        dump.start()
        dump.wait()
        stage(bexp_ref[0], 0, n_chunks, 0)
        _gather_rows(h_hbm, g0_ref, x3.at[0], gsem.at[0])
        _gather_rows(h_hbm, g1_ref, x3.at[1], gsem.at[1])

    @pl.when(j < nu)
    def _():
        xs = j % 3
        ys = j % 2
        ws = wslot_ref[j]
        nxt = (j + 2) % 3
        nq = EXPERT_F_CHUNKS
        fq, rq = f // nq, blk // nq
        _wait_rows(h_hbm, x3.at[xs], gsem.at[xs])

        x = x3[xs].astype(BF16)
        for q in range(nq):
            for r in range(q * rq, (q + 1) * rq):
                pltpu.make_async_copy(y2.at[1 - ys, pl.ds(r, 1)],
                                      o_hbm.at[pl.ds(s_ref[0, 0, r], 1)], ssem.at[1 - ys]).start()
                pltpu.make_async_copy(h_hbm.at[pl.ds(g2_ref[0, 0, r], 1)],
                                      x3.at[nxt, pl.ds(r, 1)], gsem.at[nxt]).start()
            fs = slice(q * fq, (q + 1) * fq)
            a = jnp.dot(x, wb1[ws, :, fs], preferred_element_type=F32)
            b = jnp.dot(x, wb3[ws, :, fs], preferred_element_type=F32)
            hm_scr[:, fs] = ((a * jax.nn.sigmoid(a)) * b).astype(BF16)

        @pl.when(j >= 1)
        def _():
            _wait_rows(o_hbm, y2.at[ys], ssem.at[ys])

        y2[ys] = jnp.dot(hm_scr[...], wb2[ws], preferred_element_type=F32)

        stage(se_ref[j], sc0_ref[j], scn_ref[j], 1 - ws)

        @pl.when(j == nu - 1)
        def _():
            _scatter_rows(y2.at[ys], sl_ref, o_hbm, ssem.at[ys])
            _wait_rows(o_hbm, y2.at[ys], ssem.at[ys])
            _wait_rows(o_hbm, y2.at[1 - ys], ssem.at[1 - ys])
            _wait_rows(h_hbm, x3.at[nxt], gsem.at[nxt])
            _wait_rows(h_hbm, x3.at[(j + 1) % 3], gsem.at[(j + 1) % 3])


def _expert_plan(block_exp, counts_padded, n_used, n_chunks, blk):
    ne = counts_padded.shape[0]
    nb = block_exp.shape[0]
    nblk = counts_padded // blk
    first = (jnp.cumsum(counts_padded) - counts_padded) // blk
    nonempty = nblk > 0
    ordinal = jnp.cumsum(nonempty.astype(jnp.int32)) - nonempty.astype(jnp.int32)
    cand = jnp.where(nonempty, jnp.arange(ne, dtype=jnp.int32), ne)
    suffix_min = jnp.flip(lax.cummin(jnp.flip(cand)))
    nxt = jnp.concatenate([suffix_min[1:], jnp.full((1,), ne, jnp.int32)])
    jj = jnp.arange(nb, dtype=jnp.int32)
    onehot = block_exp[:, None] == jnp.arange(ne, dtype=jnp.int32)[None, :]
    pick = lambda tab: jnp.sum(jnp.where(onehot, tab.astype(jnp.int32)[None, :], 0), axis=1)
    nxt_b = pick(nxt)
    has_b = nxt_b < ne
    m = jj - pick(first)
    n_e = jnp.maximum(pick(nblk), 1)
    quota = (n_chunks + n_e - 1) // n_e
    c0 = jnp.clip(m * quota, 0, n_chunks)
    cn = jnp.clip(n_chunks - c0, 0, quota)
    cn = jnp.where(has_b & (jj < n_used[0]), cn, 0)
    stage_e = jnp.where(has_b, nxt_b, 0)
    wslot = pick(ordinal) % 2
    i32 = lambda v: v.astype(jnp.int32)
    return i32(wslot), i32(stage_e), i32(c0), i32(cn)


def _experts(h2, row_tok, row_dst, block_exp, counts_padded, n_used, w1, w3, w2):
    t, d = h2.shape
    nb = block_exp.shape[0]
    blk = MOE_BLOCK
    f = w1.shape[2]
    rows_a = min(d, max(8, WEIGHT_CHUNK_ELEMS // f))
    rows_b = min(f, max(8, WEIGHT_CHUNK_ELEMS // d))
    n_chunks = 2 * (d // rows_a) + f // rows_b
    wslot, stage_e, c0, cn = _expert_plan(block_exp, counts_padded, n_used, n_chunks, blk)
    gidx = row_tok.reshape(nb, 1, blk)
    dump1 = 2 * t + blk + jnp.arange(blk, dtype=jnp.int32)
    sidx = jnp.concatenate([dump1, row_dst]).reshape(nb + 1, 1, blk)
    last = nb - 1
    smem = lambda m: pl.BlockSpec((1, 1, blk), m, memory_space=pltpu.SMEM)
    anyspec = pl.BlockSpec(memory_space=pl.ANY)
    grid_spec = pltpu.PrefetchScalarGridSpec(
        num_scalar_prefetch=6,
        grid=(nb,),
        in_specs=[smem(lambda j, *_: (0, 0, 0)),
                  smem(lambda j, *_: (min(1, last), 0, 0)),
                  smem(lambda j, *_: (jnp.minimum(j + 2, last), 0, 0)),
                  smem(lambda j, *_: (j, 0, 0)),
                  smem(lambda j, *_: (j + 1, 0, 0)),
                  anyspec, anyspec, anyspec, anyspec],
        out_specs=anyspec,
        scratch_shapes=[pltpu.VMEM((3, blk, d), F32),
                        pltpu.VMEM((2, blk, d), F32),
                        pltpu.VMEM((blk, f), BF16),
                        pltpu.VMEM((2, d, f), BF16), pltpu.VMEM((2, d, f), BF16),
                        pltpu.VMEM((2, f, d), BF16),
                        pltpu.VMEM((WEIGHT_CHUNKS_IN_FLIGHT, rows_a, f), F32),
                        pltpu.VMEM((WEIGHT_CHUNKS_IN_FLIGHT, rows_b, d), F32),
                        pltpu.SemaphoreType.DMA((3,)), pltpu.SemaphoreType.DMA((2,)),
                        pltpu.SemaphoreType.DMA((2 * WEIGHT_CHUNKS_IN_FLIGHT,))],
    )
    return pl.pallas_call(
        _expert_kernel,
        grid_spec=grid_spec,
        out_shape=jax.ShapeDtypeStruct((2 * t + 2 * blk, d), F32),
        compiler_params=_params(("arbitrary",)),
        name="experts",
    )(block_exp, n_used, wslot, stage_e, c0, cn, gidx, gidx, gidx, sidx, sidx, h2, w1, w3, w2)


def _final_kernel(p0_ref, p1_ref, xm_ref, w_ref, g2_ref, lg_ref, lb_ref, o_ref, *, alpha, rc):
    g2 = g2_ref[0]
    lg = lg_ref[...]
    lb = lb_ref[...]

    for r in range(xm_ref.shape[0] // rc):
        rows = pl.ds(r * rc, rc)
        w = w_ref[rows, :]
        y2 = w[:, 0:1] * p0_ref[rows, :] + w[:, 1:2] * p1_ref[rows, :]
        o_ref[rows, :] = _ln_rows(alpha * xm_ref[rows, :] + g2 * y2) * lg + lb


def _final(packed, x_mid, wts, gate2, l2g, l2b, seq, alpha, tm=512):
    t, d = x_mid.shape
    nt = t // tm
    kern = functools.partial(_final_kernel, alpha=alpha, rc=32)
    full = lambda a: pl.BlockSpec(a.shape, lambda i: (0,) * a.ndim)
    return pl.pallas_call(
        kern,
        grid=(nt,),
        in_specs=[pl.BlockSpec((tm, d), lambda i: (i, 0)),
                  pl.BlockSpec((tm, d), lambda i: (i + nt, 0)),
                  pl.BlockSpec((tm, d), lambda i: (i, 0)),
                  pl.BlockSpec((tm, 2), lambda i: (i, 0)),
                  pl.BlockSpec((1, 1, d), lambda i: ((i * tm) // seq, 0, 0)),
                  full(l2g), full(l2b)],
        out_specs=pl.BlockSpec((tm, d), lambda i: (i, 0)),
        out_shape=jax.ShapeDtypeStruct((t, d), F32),
        compiler_params=_params(("arbitrary",)),
        name="final",
    )(packed, packed, x_mid, wts, gate2, l2g, l2b)


def kernel(x, c, ctx, c_ctx, w_ada, b_ada, w_in, w_f_proj, lam_re, lam_im, log_dt, b_re, b_im,
           c_re, c_im, d_skip, w_glu, b_glu, w_s_proj, w_out, ln1_g, ln1_b, w_group, b_group,
           w_expert, b_expert, w1, w3, w2, ln2_g, ln2_b):
    depth = w_ada.shape[0]
    assert depth == 1, "single-layer block"
    bsz, seq, d = x.shape
    ctx_len = ctx.shape[1]
    t = bsz * seq
    fw = w_f_proj.shape[1]
    sw = w_s_proj.shape[1]
    gd = fw // F_GROUPS
    rows = seq // GRID_W
    alpha = (2.0 * depth) ** 0.25
    row2 = lambda v: v.reshape(1, -1).astype(F32)

    cc = jnp.concatenate([c, c_ctx[None], jnp.zeros((8 - bsz - 1, d), F32)], axis=0)
    mod = _adaln(cc, w_ada[0], b_ada[0].reshape(1, -1))
    mods = [mod[:, k * d:(k + 1) * d] for k in range(N_MOD)]
    per_b = lambda m: m[:bsz].reshape(bsz, 1, d)
    ctx_row = lambda m: m[bsz:bsz + 1].reshape(1, 1, d)

    wi = w_in[0].astype(BF16)
    tn = sw
    n_sig = (2 * d) // tn
    main, us = _ln_mod_matmul(x.reshape(t, d), per_b(mods[0]), per_b(mods[1]), wi,
                              rows_per_mod=seq, n_sig=n_sig, tm=1024, tn=tn,
                              nj=wi.shape[1] // tn, col_shift=(fw + sw) // tn)
    _, us_ctx = _ln_mod_matmul(ctx.reshape(bsz * ctx_len, d), ctx_row(mods[0]), ctx_row(mods[1]),
                               wi, rows_per_mod=bsz * ctx_len, n_sig=0, tm=ctx_len, tn=tn,
                               nj=1, col_shift=fw // tn)

    f_out = _fourier_mix(main, 2 * d, bsz, rows, GRID_W, gd)

    perm = _octet_perm()
    tables = _s5_tables(lam_re[0], lam_im[0], log_dt[0], b_re[0], b_im[0], c_re[0], c_im[0],
                        d_skip[0])
    zl = _relayout_in(us, perm)
    zc = _relayout_in(us_ctx, perm)
    y_chunks = _s5_core(zl, zc, tables, bsz)
    y_s = _relayout_out(y_chunks, perm)

    wr = jnp.concatenate([w_group[0], w_expert[0],
                          jnp.zeros((d, ROUTE_LANES - N_EXPERT_GROUPS - N_EXPERTS), F32)], axis=1)
    br = jnp.concatenate([b_group[0], b_expert[0],
                          jnp.zeros((ROUTE_LANES - N_EXPERT_GROUPS - N_EXPERTS,), F32)])
    x_mid, h2, logits = _merge(
        f_out, y_s, main, x.reshape(t, d), per_b(mods[2]), per_b(mods[3]), per_b(mods[4]),
        w_glu[0].astype(BF16), row2(b_glu[0]), w_f_proj[0].astype(BF16), w_s_proj[0].astype(BF16),
        w_out[0].astype(BF16), row2(ln1_g[0]), row2(ln1_b[0]), wr.astype(BF16), row2(br),
        seq, alpha)

    idx_t, wts, cnt = _route(logits)
    experts = idx_t[0:2].astype(jnp.int32)
    rank = idx_t[2:4].astype(jnp.int32)
    counts = cnt[0, :N_EXPERTS].astype(jnp.int32)
    blk = MOE_BLOCK
    n_assign = 2 * t
    nb = -(-(n_assign + N_EXPERTS * (blk - 1)) // blk)
    padded = (counts + blk - 1) // blk * blk
    pend = jnp.cumsum(padded)
    pstart = pend - padded
    eids = jnp.arange(N_EXPERTS, dtype=jnp.int32)[:, None, None]
    dest = jnp.sum(jnp.where(experts[None] == eids, pstart[:, None, None], 0), axis=0) + rank
    n_used = (pend[-1] // blk).astype(jnp.int32).reshape(1)
    block_row = jnp.arange(nb, dtype=jnp.int32) * blk
    block_exp = jnp.minimum(jnp.sum((pend[None, :] <= block_row[:, None]).astype(jnp.int32), axis=1),
                            N_EXPERTS - 1)
    assign = jnp.zeros((nb * blk,), jnp.int32).at[dest.reshape(-1)].add(
        jnp.arange(1, n_assign + 1, dtype=jnp.int32)) - 1
    valid = assign >= 0
    row_tok = jnp.where(valid, assign % t, 0)
    row_in_2blk = jnp.arange(nb * blk, dtype=jnp.int32) % (2 * blk)
    row_dst = jnp.where(valid, assign, 2 * t + row_in_2blk)

    packed = _experts(h2, row_tok, row_dst, block_exp, padded, n_used, w1[0], w3[0], w2[0])
    out = _final(packed, x_mid, wts, per_b(mods[5]), row2(ln2_g[0]), row2(ln2_b[0]), seq, alpha)
    return out.reshape(bsz, seq, d)
```

```python
import functools
import math

import numpy as np
import jax
import jax.numpy as jnp
from jax import lax
from jax.experimental import pallas as pl
from jax.experimental.pallas import tpu as pltpu

F32 = jnp.float32
BF16 = jnp.bfloat16

GRID_W = 64
F_GROUPS = 4
S_GROUP_DIM = 16
S_STATE = 64
N_EXPERT_GROUPS = 4
EXPERTS_PER_GROUP = 8
N_EXPERTS = N_EXPERT_GROUPS * EXPERTS_PER_GROUP
N_MOD = 6
LN_EPS = 1e-5

LANES = 128
S5_CHUNK = 64
OCTET = LANES // S_GROUP_DIM
FOURIER_COL_GROUP = 8
MOE_BLOCK = 256
WEIGHT_CHUNK_ELEMS = 256 * 1024
WEIGHT_CHUNKS_IN_FLIGHT = 4
EXPERT_F_CHUNKS = 4
ROUTE_LANES = 128
VMEM_LIMIT = 56 * 1024 * 1024


def _params(sem, vmem=VMEM_LIMIT):
    return pltpu.CompilerParams(dimension_semantics=sem, vmem_limit_bytes=vmem)


def _ln_rows(x):
    mu = jnp.mean(x, axis=-1, keepdims=True)
    xc = x - mu
    var = jnp.mean(xc * xc, axis=-1, keepdims=True)
    return xc * lax.rsqrt(var + LN_EPS)


def _adaln_kernel(c_ref, w_ref, b_ref, o_ref):
    s = c_ref[...]
    s = s * jax.nn.sigmoid(s)
    o_ref[...] = jnp.dot(s.astype(BF16), w_ref[...].astype(BF16),
                         preferred_element_type=F32) + b_ref[...]


def _adaln(cc, w, b, tn=1024):
    m, d = cc.shape
    n = w.shape[1]
    return pl.pallas_call(
        _adaln_kernel,
        grid=(n // tn,),
        in_specs=[pl.BlockSpec((m, d), lambda j: (0, 0)),
                  pl.BlockSpec((d, tn), lambda j: (0, j)),
                  pl.BlockSpec((1, tn), lambda j: (0, j))],
        out_specs=pl.BlockSpec((m, tn), lambda j: (0, j)),
        out_shape=jax.ShapeDtypeStruct((m, n), F32),
        compiler_params=_params(("arbitrary",)),
        name="adaln",
    )(cc, w, b)


def _ln_mod_rows(x_ref, rows, sc1, sh):
    mu = jnp.mean(x_ref[rows, :], axis=-1, keepdims=True)
    xc = x_ref[rows, :] - mu
    var = jnp.mean(xc * xc, axis=-1, keepdims=True)
    return (x_ref[rows, :] - mu) * lax.rsqrt(var + LN_EPS) * sc1 + sh


def _lnmm_kernel(x0_ref, sh0_ref, sc0_ref, xn_ref, shn_ref, scn_ref, w_ref, o_ref, us_ref,
                 ha, hb, *, n_sig, nrc, mchunks):
    i = pl.program_id(0)
    j = pl.program_id(1)
    tm = xn_ref.shape[0]
    rcs = tm // nrc

    @pl.when((i == 0) & (j == 0))
    def _():
        def body(r, carry):
            rows = pl.ds(pl.multiple_of(r * rcs, rcs), rcs)
            ha[rows, :] = _ln_mod_rows(x0_ref, rows, 1.0 + sc0_ref[0], sh0_ref[0]).astype(BF16)
            return carry

        lax.fori_loop(0, nrc, body, 0)

    def step(h_cur, h_nxt):
        tmc = tm // mchunks
        for m in range(mchunks):
            rs = pl.ds(m * tmc, tmc)
            acc = jnp.dot(h_cur[rs, :], w_ref[...], preferred_element_type=F32)
            o_ref[rs, :] = jnp.where(j < n_sig, jax.nn.sigmoid(acc), acc).astype(o_ref.dtype)
            us_ref[rs, :] = acc
        c = jnp.minimum(j, nrc - 1)
        rows = pl.ds(pl.multiple_of(c * rcs, rcs), rcs)
        h_nxt[rows, :] = _ln_mod_rows(xn_ref, rows, 1.0 + scn_ref[0], shn_ref[0]).astype(BF16)

    @pl.when(i % 2 == 0)
    def _():
        step(ha, hb)

    @pl.when(i % 2 == 1)
    def _():
        step(hb, ha)


def _ln_mod_matmul(x2, shift, scale, w, rows_per_mod, n_sig, tm, tn, nj, col_shift):
    t, d = x2.shape
    nw = w.shape[1] // tn
    n = nj * tn
    nt = t // tm
    nrc = min(8, nj)
    kern = functools.partial(_lnmm_kernel, n_sig=n_sig, nrc=nrc, mchunks=max(1, tm // 256))
    nxt = lambda i: jnp.minimum(i + 1, nt - 1)
    once = pl.Buffered(1)
    mod0 = pl.BlockSpec((1, 1, d), lambda i, j: (0, 0, 0), pipeline_mode=once)
    modn = pl.BlockSpec((1, 1, d), lambda i, j: ((nxt(i) * tm) // rows_per_mod, 0, 0))
    return pl.pallas_call(
        kern,
        grid=(nt, nj),
        in_specs=[pl.BlockSpec((tm, d), lambda i, j: (0, 0), pipeline_mode=once), mod0, mod0,
                  pl.BlockSpec((tm, d), lambda i, j: (nxt(i), 0)), modn, modn,
                  pl.BlockSpec((d, tn), lambda i, j: (0, (j + col_shift) % nw))],
        out_specs=[pl.BlockSpec((tm, tn), lambda i, j: (i, j)),
                   pl.BlockSpec((tm, tn), lambda i, j: (i, 0))],
        out_shape=[jax.ShapeDtypeStruct((t, n), BF16),
                   jax.ShapeDtypeStruct((t, tn), F32)],
        scratch_shapes=[pltpu.VMEM((tm, d), BF16), pltpu.VMEM((tm, d), BF16)],
        compiler_params=_params(("arbitrary", "arbitrary")),
        name="ln_mod_matmul",
    )(x2, shift, scale, x2, shift, scale, w)


def _dft_cs(n):
    k = np.arange(n)
    ang = 2.0 * np.pi * np.outer(k, k) / n
    return np.cos(ang), np.sin(ang)


def _fourier_tables(rows, cols, gd):
    c1, s1 = _dft_cs(gd)
    half = gd // 2
    w1 = np.stack([np.concatenate([c1[:, h * half:(h + 1) * half],
                                   -s1[:, h * half:(h + 1) * half]], axis=1)
                   for h in range(2)]) / math.sqrt(gd)
    c2, s2 = _dft_cs(cols)
    rep = 256 // cols
    eye = np.eye(rep)
    l2 = np.concatenate([np.kron(eye, c2), np.kron(eye, s2)], axis=0) / math.sqrt(cols)
    c3, s3 = _dft_cs(rows)
    m3 = np.concatenate([c3, s3], axis=0)
    return tuple(jnp.asarray(a, F32).astype(BF16) for a in (w1, l2, m3))


def _fourier_kernel(x_ref, w1_ref, l2_ref, m3_ref, o_ref, a_scr, b_scr, *, rows, cols, scale3):
    npos = rows * cols
    half = o_ref.shape[2]
    slab = l2_ref.shape[1]
    rps = slab // cols
    rc = 1024 if npos % 1024 == 0 else npos

    def s1(r, carry):
        rs = pl.ds(pl.multiple_of(r * rc, rc), rc)
        a_scr[rs, :] = jnp.dot(x_ref[rs, :], w1_ref[0],
                               preferred_element_type=F32).astype(BF16)
        return carry

    lax.fori_loop(0, npos // rc, s1, 0)

    unroll = 4 if (npos // slab) % 4 == 0 else 1

    def s2(s, carry):
        for k in range(unroll):
            si = s * unroll + k
            rs = pl.ds(pl.multiple_of(si * slab, slab), slab)
            g = jnp.dot(l2_ref[...], a_scr[rs, :], preferred_element_type=F32)
            rr = pl.ds(si * rps, rps)
            b_scr[0, rr] = (g[0:slab, 0:half] + g[slab:, half:]).reshape(rps, cols, half)
            b_scr[1, rr] = (g[0:slab, half:] - g[slab:, 0:half]).reshape(rps, cols, half)
        return carry

    lax.fori_loop(0, npos // slab // unroll, s2, 0)

    cg = FOURIER_COL_GROUP
    for c0 in range(0, cols, cg):
        xr = jnp.swapaxes(b_scr[0, :, c0:c0 + cg, :], 0, 1)
        xi = jnp.swapaxes(b_scr[1, :, c0:c0 + cg, :], 0, 1)
        bc = jnp.concatenate([p for k in range(cg) for p in (xr[k], xi[k])],
                             axis=1).astype(BF16)
        h = jnp.dot(m3_ref[...], bc, preferred_element_type=F32)
        ys = [h[0:rows, 2 * k * half:(2 * k + 1) * half]
              + h[rows:, (2 * k + 1) * half:(2 * k + 2) * half] for k in range(cg)]
        o_ref[:, c0:c0 + cg, :] = jnp.swapaxes(jnp.stack(ys, axis=0), 0, 1) * scale3


def _fourier_mix(u_all, col0, batch, rows, cols, gd):
    npos = rows * cols
    half = gd // 2
    w1, l2, m3 = _fourier_tables(rows, cols, gd)
    kern = functools.partial(_fourier_kernel, rows=rows, cols=cols, scale3=1.0 / math.sqrt(rows))
    cb0 = col0 // gd
    return pl.pallas_call(
        kern,
        grid=(batch, F_GROUPS, 2),
        in_specs=[pl.BlockSpec((npos, gd), lambda b, g, h: (b, cb0 + g)),
                  pl.BlockSpec((1, gd, gd), lambda b, g, h: (h, 0, 0)),
                  pl.BlockSpec(l2.shape, lambda b, g, h: (0, 0)),
                  pl.BlockSpec(m3.shape, lambda b, g, h: (0, 0))],
        out_specs=pl.BlockSpec((rows, cols, half), lambda b, g, h: (b, 0, g * 2 + h)),
        out_shape=jax.ShapeDtypeStruct((batch * rows, cols, F_GROUPS * gd), F32),
        scratch_shapes=[pltpu.VMEM((npos, gd), BF16), pltpu.VMEM((2, rows, cols, half), F32)],
        compiler_params=_params(("arbitrary", "arbitrary", "arbitrary")),
        name="fourier_mix",
    )(u_all, w1, l2, m3).reshape(batch * npos, F_GROUPS * gd)


def _octet_perm():
    n = OCTET * LANES
    p = np.zeros((n, n), np.float32)
    for k in range(OCTET):
        for g in range(OCTET):
            for h in range(S_GROUP_DIM):
                p[k * LANES + g * S_GROUP_DIM + h, g * LANES + k * S_GROUP_DIM + h] = 1.0
    return jnp.asarray(p, BF16)


def _relayout_in_kernel(x_ref, p_ref, o_ref):
    io = pl.program_id(1)
    tiles = x_ref[:, pl.ds(pl.multiple_of(io * OCTET, OCTET), OCTET), :]
    by_pos = jnp.swapaxes(tiles, 0, 1)
    a = jnp.concatenate([by_pos[k] for k in range(OCTET)], axis=1).astype(BF16)
    b = jnp.dot(a, p_ref[...], preferred_element_type=F32).astype(BF16)
    for g in range(OCTET):
        o_ref[g] = b[:, g * LANES:(g + 1) * LANES]


def _relayout_in(us, perm):
    t, sw = us.shape
    nchunk = t // S5_CHUNK
    ngroups = sw // S_GROUP_DIM
    return pl.pallas_call(
        _relayout_in_kernel,
        grid=(sw // LANES, S5_CHUNK // OCTET),
        in_specs=[pl.BlockSpec((nchunk, S5_CHUNK, LANES), lambda o, io: (0, 0, o)),
                  pl.BlockSpec(perm.shape, lambda o, io: (0, 0))],
        out_specs=pl.BlockSpec((OCTET, nchunk, LANES), lambda o, io: (o, 0, io)),
        out_shape=jax.ShapeDtypeStruct((ngroups, nchunk, S5_CHUNK * S_GROUP_DIM), BF16),
        compiler_params=_params(("arbitrary", "arbitrary")),
        name="s5_relayout_in",
    )(us.reshape(nchunk, S5_CHUNK, sw), perm)


def _relayout_out_kernel(y_ref, p_ref, o_ref):
    io = pl.program_id(1)
    b = jnp.concatenate([y_ref[g] for g in range(OCTET)], axis=1)
    a = jnp.dot(b, p_ref[...], preferred_element_type=F32)
    by_pos = jnp.stack([a[:, k * LANES:(k + 1) * LANES] for k in range(OCTET)], axis=0)
    o_ref[:, pl.ds(pl.multiple_of(io * OCTET, OCTET), OCTET), :] = jnp.swapaxes(by_pos, 0, 1)


def _relayout_out(y, perm):
    ngroups, nchunk, _ = y.shape
    t = nchunk * S5_CHUNK
    sw = ngroups * S_GROUP_DIM
    return pl.pallas_call(
        _relayout_out_kernel,
        grid=(sw // LANES, S5_CHUNK // OCTET),
        in_specs=[pl.BlockSpec((OCTET, nchunk, LANES), lambda o, io: (o, 0, io)),
                  pl.BlockSpec(perm.shape, lambda o, io: (0, 0))],
        out_specs=pl.BlockSpec((nchunk, S5_CHUNK, LANES), lambda o, io: (0, 0, o)),
        out_shape=jax.ShapeDtypeStruct((nchunk, S5_CHUNK, sw), F32),
        compiler_params=_params(("arbitrary", "arbitrary")),
        name="s5_relayout_out",
    )(y, perm).reshape(t, sw)


def _s5_tables(lam_re, lam_im, log_dt, b_re, b_im, c_re, c_im, d_skip):
    lc = S5_CHUNK

    e_all = jnp.arange(-(lc - 1), lc + 1, dtype=F32)

    def direction(d):
        lr, li = lam_re[d].astype(F32), lam_im[d].astype(F32)
        dt = jnp.exp(log_dt[d].astype(F32))[:, None]
        zr, zi = lr * dt, li * dt
        mag = jnp.exp(zr)
        nr, ni = mag * jnp.cos(zi) - 1.0, mag * jnp.sin(zi)
        den = lr * lr + li * li
        qr, qi = (nr * lr + ni * li) / den, (ni * lr - nr * li) / den
        br, bi = b_re[d].astype(F32), b_im[d].astype(F32)
        bbr = qr[:, :, None] * br - qi[:, :, None] * bi
        bbi = qr[:, :, None] * bi + qi[:, :, None] * br
        bbar = (jnp.swapaxes(bbr, 1, 2), jnp.swapaxes(bbi, 1, 2))
        cc = (c_re[d].astype(F32), c_im[d].astype(F32))
        pm = jnp.exp(zr[:, None, :] * e_all[None, :, None])
        ang = zi[:, None, :] * e_all[None, :, None]
        return (pm * jnp.cos(ang), pm * jnp.sin(ang)), bbar, cc

    powf, bf, cf = direction(0)
    powb, bb, cb = direction(1)

    def up(t, lo):
        return (t[0][:, lo + lc - 1:lo + 2 * lc - 1], t[1][:, lo + lc - 1:lo + 2 * lc - 1])

    def down(t, hi):
        return (jnp.flip(t[0][:, hi:hi + lc], axis=1), jnp.flip(t[1][:, hi:hi + lc], axis=1))

    cat = lambda x, y: jnp.concatenate([x, y], axis=-1)
    dup = lambda t: [cat(t[0], t[0]), cat(t[1], t[1])]
    f_d0, f_u0, b_u0, b_d0 = down(powf, 0), up(powf, 0), up(powb, 0), down(powb, 0)
    f_dl, f_u1, b_dl = down(powf, lc - 1), up(powf, 1), down(powb, lc)
    pow_fac = jnp.stack(dup(f_d0) + dup(b_u0) + dup(f_u0) + dup(b_d0)
                        + [cat(f_dl[0], b_u0[0]), cat(f_dl[1], b_u0[1])]
                        + [cat(f_u1[0], b_dl[0]), cat(f_u1[1], b_dl[1])], axis=0)
    (bfr, bfi), (bbr, bbi) = bf, bb
    (cfr, cfi), (cbr, cbi) = cf, cb
    vec_fac = jnp.stack([cat(bfr, -bfi), cat(-bfi, -bfr),
                         cat(bbr, -bbi), cat(-bbi, -bbr),
                         cat(cfr, cfi), cat(-cfi, cfr),
                         cat(cbr, cbi), cat(-cbi, cbr),
                         cat(bfr, bbr), cat(-bfi, -bbi),
                         cat(bfi, bbi), cat(bfr, bbr),
                         cat(cfr, cbr), cat(-cfi, -cbi),
                         cat(-cfi, -cbi), cat(-cfr, -cbr)], axis=0)
    last = 2 * lc - 1
    decay = jnp.stack([jnp.concatenate([powf[0][:, last], powb[0][:, last]], axis=1),
                       jnp.concatenate([powf[1][:, last], powb[1][:, last]], axis=1)],
                      axis=1)
    dvec = jnp.tile(d_skip.astype(F32), (1, lc))[:, None, :]
    return pow_fac, vec_fac, decay, dvec


_NT = (((1,), (1,)), ((), ()))


def _s5_kernel(z_ref, zc_ref, pf_ref, vf_ref, dec_ref, dv_ref,
               y_ref, e_scr, ec_scr, srf_scr, srb_scr, sif_scr, sib_scr, *, batch, nlat, nctx):
    p2 = 2 * S_STATE
    z = z_ref[0]

    def table(ip, iv):
        pa, pb = pf_ref[ip, 0], pf_ref[ip + 1, 0]
        va, vb = vf_ref[iv, 0], vf_ref[iv + 1, 0]
        t = pa[:, None, :] * va[None, :, :] + pb[:, None, :] * vb[None, :, :]
        return t.reshape(pa.shape[0] * va.shape[0], pa.shape[1])

    pmat = jnp.concatenate([table(8, 8), table(8, 10)], axis=1).astype(BF16)
    e = jnp.dot(z, pmat, preferred_element_type=F32)
    ec = jnp.dot(zc_ref[0], pmat, preferred_element_type=F32)
    for col in range(2):
        e_scr[col] = e[:, col * p2:(col + 1) * p2]
        ec_scr[col] = ec[:, col * p2:(col + 1) * p2]

    ar = dec_ref[0, 0:1, :]
    ai = dec_ref[0, 1:2, :]
    lane = lax.broadcasted_iota(jnp.int32, (batch, p2), 1)
    is_f = lane < S_STATE

    def advance(sr, si, er, ei):
        return ar * sr - ai * si + er, ar * si + ai * sr + ei

    def pick(ref, nf, nb, n, col):
        f = ref[col, pl.ds(nf, batch, stride=n), :]
        b = ref[col, pl.ds(nb, batch, stride=n), :]
        return jnp.where(is_f, f, b)

    sr = jnp.zeros((batch, p2), F32)
    si = jnp.zeros((batch, p2), F32)
    for k in range(nctx):
        nf, nb = k, nctx - 1 - k
        sr, si = advance(sr, si, pick(ec_scr, nf, nb, nctx, 0), pick(ec_scr, nf, nb, nctx, 1))
    for k in range(nlat):
        nf, nb = k, nlat - 1 - k
        rf = pl.ds(nf, batch, stride=nlat)
        rb = pl.ds(nb, batch, stride=nlat)
        srf_scr[rf, :] = sr
        srb_scr[rb, :] = sr
        sif_scr[rf, :] = si
        sib_scr[rb, :] = si
        sr, si = advance(sr, si, pick(e_scr, nf, nb, nlat, 0), pick(e_scr, nf, nb, nlat, 1))
    lane_all = lax.broadcasted_iota(jnp.int32, srf_scr.shape, 1)
    s_re = jnp.where(lane_all < S_STATE, srf_scr[...], srb_scr[...])
    s_im = jnp.where(lane_all < S_STATE, sif_scr[...], sib_scr[...])
    s_all = jnp.concatenate([s_re, s_im], axis=1).astype(BF16)

    n = z.shape[1]
    tf = lax.dot_general(table(0, 0).astype(BF16), table(4, 4).astype(BF16), _NT,
                         preferred_element_type=F32)
    tb = lax.dot_general(table(2, 2).astype(BF16), table(6, 6).astype(BF16), _NT,
                         preferred_element_type=F32)
    ri = lax.broadcasted_iota(jnp.int32, (n, n), 0)
    ci = lax.broadcasted_iota(jnp.int32, (n, n), 1)
    rpos = ri // S_GROUP_DIM
    cpos = ci // S_GROUP_DIM
    tmat = (jnp.where(rpos <= cpos, tf, 0.0) + jnp.where(rpos >= cpos, tb, 0.0)
            + jnp.where(ri == ci, dv_ref[0], 0.0))
    y = jnp.dot(z, tmat.astype(BF16), preferred_element_type=F32)
    qmat_t = jnp.concatenate([table(10, 12), table(10, 14)], axis=1).astype(BF16)
    y = y + lax.dot_general(s_all, qmat_t, _NT, preferred_element_type=F32)
    y_ref[0] = y.astype(y_ref.dtype)


def _s5_core(zl, zc, tables, batch):
    pow_fac, vec_fac, decay, dvec = tables
    g, rl, n = zl.shape
    rcx = zc.shape[1]
    nlat, nctx = rl // batch, rcx // batch
    kern = functools.partial(_s5_kernel, batch=batch, nlat=nlat, nctx=nctx)
    gmap = lambda i: (i, 0, 0)
    spec = lambda a: pl.BlockSpec((1,) + a.shape[1:], lambda i: (i,) + (0,) * (a.ndim - 1))
    fac_spec = lambda a: pl.BlockSpec((a.shape[0], 1) + a.shape[2:], lambda i: (0, i, 0, 0))
    return pl.pallas_call(
        kern,
        grid=(g,),
        in_specs=[spec(zl), spec(zc), fac_spec(pow_fac), fac_spec(vec_fac), spec(decay), spec(dvec)],
        out_specs=pl.BlockSpec((1, rl, n), gmap),
        out_shape=jax.ShapeDtypeStruct((g, rl, n), BF16),
        scratch_shapes=[pltpu.VMEM((2, rl, 2 * S_STATE), F32),
                        pltpu.VMEM((2, rcx, 2 * S_STATE), F32)]
                       + [pltpu.VMEM((rl, 2 * S_STATE), F32)] * 4,
        compiler_params=_params(("arbitrary",)),
        name="s5_core",
    )(zl, zc, pow_fac, vec_fac, decay, dvec)


def _merge_kernel(fo_ref, ys_ref, zf_ref, zs_ref, x_ref, g1_ref, sh2_ref, sc2_ref,
                  wglu_ref, bglu_ref, wf_ref, ws_ref, wo_ref, l1g_ref, l1b_ref, wr_ref, br_ref,
                  xm_ref, h2_ref, lg_ref, mix_a, mix_b, h2_scr, *, alpha, rc):
    i = pl.program_id(0)

    @pl.when(i == 0)
    def _():
        mix_b[...] = jnp.zeros_like(mix_b)

    def step(mix_cur, mix_prv):
        y = jax.nn.gelu(ys_ref[...], approximate=True)
        glu = jnp.dot(y.astype(BF16), wglu_ref[...], preferred_element_type=F32) + bglu_ref[...]
        s_out = y * jax.nn.sigmoid(glu)
        a = jnp.dot(fo_ref[...].astype(BF16), wf_ref[...], preferred_element_type=F32)
        b = jnp.dot(s_out.astype(BF16), ws_ref[...], preferred_element_type=F32)
        m = zf_ref[...].astype(F32) * a + zs_ref[...].astype(F32) * b
        mix_cur[...] = jnp.dot(m.astype(BF16), wo_ref[...], preferred_element_type=F32)

        g1 = g1_ref[0]
        sh2 = sh2_ref[0]
        sc2 = 1.0 + sc2_ref[0]
        l1g = l1g_ref[...]
        l1b = l1b_ref[...]
        for r in range(x_ref.shape[0] // rc):
            rows = pl.ds(r * rc, rc)
            xm = _ln_rows(alpha * x_ref[rows, :] + g1 * mix_prv[rows, :]) * l1g + l1b
            xm_ref[rows, :] = xm
            h2 = _ln_rows(xm) * sc2 + sh2
            h2_ref[rows, :] = h2
            h2_scr[rows, :] = h2.astype(BF16)
        lg_ref[...] = jnp.dot(h2_scr[...], wr_ref[...], preferred_element_type=F32) + br_ref[...]

    @pl.when(i % 2 == 0)
    def _():
        step(mix_a, mix_b)

    @pl.when(i % 2 == 1)
    def _():
        step(mix_b, mix_a)


def _merge(f_out, y_s, main, x2, gate1, shift2, scale2, wglu, bglu, wf, ws, wo, l1g, l1b, wr, br,
           seq, alpha, tm=256):
    t, d = x2.shape
    nt = t // tm
    fw, sw = f_out.shape[1], y_s.shape[1]
    kern = functools.partial(_merge_kernel, alpha=alpha, rc=32)
    cur = lambda i: jnp.minimum(i, nt - 1)
    prv = lambda i: jnp.maximum(i - 1, 0)
    rowc = lambda w: pl.BlockSpec((tm, w), lambda i: (cur(i), 0))
    rowp = lambda w: pl.BlockSpec((tm, w), lambda i: (prv(i), 0))
    mod = pl.BlockSpec((1, 1, d), lambda i: ((prv(i) * tm) // seq, 0, 0))
    full = lambda a: pl.BlockSpec(a.shape, lambda i: (0,) * a.ndim, pipeline_mode=pl.Buffered(1))
    return pl.pallas_call(
        kern,
        grid=(nt + 1,),
        in_specs=[rowc(fw), rowc(sw),
                  pl.BlockSpec((tm, d), lambda i: (cur(i), 0)),
                  pl.BlockSpec((tm, d), lambda i: (cur(i), 1)),
                  rowp(d), mod, mod, mod,
                  full(wglu), full(bglu), full(wf), full(ws), full(wo), full(l1g), full(l1b),
                  full(wr), full(br)],
        out_specs=[rowp(d), rowp(d), rowp(ROUTE_LANES)],
        out_shape=[jax.ShapeDtypeStruct((t, d), F32), jax.ShapeDtypeStruct((t, d), F32),
                   jax.ShapeDtypeStruct((t, ROUTE_LANES), F32)],
        scratch_shapes=[pltpu.VMEM((tm, d), F32), pltpu.VMEM((tm, d), F32),
                        pltpu.VMEM((tm, d), BF16)],
        compiler_params=_params(("arbitrary",)),
        name="merge",
    )(f_out, y_s, main, main, x2, gate1, shift2, scale2, wglu, bglu, wf, ws, wo, l1g, l1b, wr, br)


def _route_kernel(lg_ref, tri_ref, idx_ref, w_ref, cnt_ref, run_scr):
    i = pl.program_id(0)

    @pl.when(i == 0)
    def _():
        run_scr[...] = jnp.zeros_like(run_scr)

    lg = lg_ref[...]
    tm = lg.shape[0]
    lane = lax.broadcasted_iota(jnp.int32, lg.shape, 1)
    neg = jnp.float32(-1e30)
    big = jnp.int32(ROUTE_LANES)
    ng, epg = N_EXPERT_GROUPS, EXPERTS_PER_GROUP

    is_g = lane < ng
    gl = jnp.where(is_g, lg, neg)
    gm = jnp.max(gl, axis=-1, keepdims=True)
    gidx = jnp.min(jnp.where(gl == gm, lane, big), axis=-1, keepdims=True)
    gsum = jnp.sum(jnp.where(is_g, jnp.exp(gl - gm), 0.0), axis=-1, keepdims=True)
    g_top = 1.0 / gsum

    lo = ng + gidx * epg
    el = jnp.where((lane >= lo) & (lane < lo + epg), lg, neg)
    m1 = jnp.max(el, axis=-1, keepdims=True)
    i1 = jnp.min(jnp.where(el == m1, lane, big), axis=-1, keepdims=True)
    el2 = jnp.where(lane == i1, neg, el)
    m2 = jnp.max(el2, axis=-1, keepdims=True)
    i2 = jnp.min(jnp.where(el2 == m2, lane, big), axis=-1, keepdims=True)
    tt = jnp.exp(m2 - m1)
    w0 = g_top / (1.0 + tt)
    w1 = g_top * tt / (1.0 + tt)
    e0 = i1 - ng
    e1 = i2 - ng

    oh = (lane == e0) | (lane == e1)
    pre = jnp.dot(tri_ref[...], jnp.where(oh, 1.0, 0.0).astype(BF16),
                  preferred_element_type=F32) + run_scr[...]
    r0 = jnp.sum(jnp.where(lane == e0, pre, 0.0), axis=-1, keepdims=True)
    r1 = jnp.sum(jnp.where(lane == e1, pre, 0.0), axis=-1, keepdims=True)
    run_scr[...] = run_scr[...] + jnp.sum(jnp.where(oh, 1.0, 0.0), axis=0, keepdims=True)
    cnt_ref[...] = run_scr[...]

    vals = (e0.astype(F32), e1.astype(F32), r0, r1)
    info = jnp.zeros(lg.shape, F32)
    for k, v in enumerate(vals):
        info = jnp.where(lane == k, v, info)
    sel = (lax.broadcasted_iota(jnp.int32, (8, ROUTE_LANES), 0)
           == lax.broadcasted_iota(jnp.int32, (8, ROUTE_LANES), 1)).astype(BF16)
    hi = jnp.floor(info * (1.0 / 256.0))
    lo = info - 256.0 * hi
    t_hi = lax.dot_general(sel, hi.astype(BF16), _NT, preferred_element_type=F32)
    t_lo = lax.dot_general(sel, lo.astype(BF16), _NT, preferred_element_type=F32)
    idx_ref[...] = 256.0 * t_hi + t_lo
    lane2 = lax.broadcasted_iota(jnp.int32, (tm, 2), 1)
    w_ref[...] = jnp.where(lane2 == 0, w0, w1)


def _route(logits, tm=512):
    t = logits.shape[0]
    assert 2 * t <= 256 * 256, "ranks are carried through the transpose as two base-256 digits"
    tri = jnp.asarray(np.tril(np.ones((tm, tm), np.float32), -1), BF16)
    return pl.pallas_call(
        _route_kernel,
        grid=(t // tm,),
        in_specs=[pl.BlockSpec((tm, ROUTE_LANES), lambda i: (i, 0)),
                  pl.BlockSpec((tm, tm), lambda i: (0, 0))],
        out_specs=[pl.BlockSpec((8, tm), lambda i: (0, i)),
                   pl.BlockSpec((tm, 2), lambda i: (i, 0)),
                   pl.BlockSpec((1, ROUTE_LANES), lambda i: (0, 0))],
        out_shape=[jax.ShapeDtypeStruct((8, t), F32),
                   jax.ShapeDtypeStruct((t, 2), F32),
                   jax.ShapeDtypeStruct((1, ROUTE_LANES), F32)],
        scratch_shapes=[pltpu.VMEM((1, ROUTE_LANES), F32)],
        compiler_params=_params(("arbitrary",)),
        name="route",
    )(logits, tri)


def _gather_rows(src_hbm, idx_ref, dst, sem):
    for r in range(dst.shape[0]):
        pltpu.make_async_copy(src_hbm.at[pl.ds(idx_ref[0, 0, r], 1)],
                              dst.at[pl.ds(r, 1)], sem).start()


def _scatter_rows(src, idx_ref, dst_hbm, sem):
    for r in range(src.shape[0]):
        pltpu.make_async_copy(src.at[pl.ds(r, 1)],
                              dst_hbm.at[pl.ds(idx_ref[0, 0, r], 1)], sem).start(priority=r % 2)


def _wait_rows(any_hbm, buf, sem):
    for r in range(buf.shape[0]):
        pltpu.make_async_copy(any_hbm.at[pl.ds(0, 1)], buf.at[pl.ds(r, 1)], sem).wait()


def _expert_kernel(bexp_ref, nu_ref, wslot_ref, se_ref, sc0_ref, scn_ref,
                   g0_ref, g1_ref, g2_ref, s_ref, sl_ref, h_hbm, w1_hbm, w3_hbm, w2_hbm,
                   o_hbm, x3, y2, hm_scr, wb1, wb3, wb2, st_a, st_b, gsem, ssem, wsem):
    j = pl.program_id(0)
    nu = nu_ref[0]
    blk, d = x3.shape[1], x3.shape[2]
    f = wb1.shape[2]
    depth = st_a.shape[0]
    rows_a = st_a.shape[1]
    rows_b = st_b.shape[1]
    n_a, n_b = d // rows_a, f // rows_b
    n_chunks = 2 * n_a + n_b

    def chunk_copy(e, c, kind):
        slot = c % depth
        if kind == 0:
            return pltpu.make_async_copy(w1_hbm.at[e, pl.ds(c * rows_a, rows_a), :],
                                         st_a.at[slot], wsem.at[slot])
        if kind == 1:
            return pltpu.make_async_copy(w3_hbm.at[e, pl.ds((c - n_a) * rows_a, rows_a), :],
                                         st_a.at[slot], wsem.at[slot])
        return pltpu.make_async_copy(w2_hbm.at[e, pl.ds((c - 2 * n_a) * rows_b, rows_b), :],
                                     st_b.at[slot], wsem.at[depth + slot])

    def kind_is(c, kind):
        lo = (0, n_a, 2 * n_a)[kind]
        hi = (n_a, 2 * n_a, n_chunks)[kind]
        return (c >= lo) & (c < hi)

    def start_chunk(e, c):
        for kind in range(3):
            @pl.when(kind_is(c, kind))
            def _():
                chunk_copy(e, c, kind).start(priority=1)

    def finish_chunk(e, c, tgt):
        slot = c % depth
        for kind in range(3):
            @pl.when(kind_is(c, kind))
            def _():
                chunk_copy(e, c, kind).wait()
                if kind == 0:
                    rs = pl.ds(pl.multiple_of(c * rows_a, rows_a), rows_a)
                    wb1[tgt, rs, :] = st_a[slot].astype(BF16)
                elif kind == 1:
                    rs = pl.ds(pl.multiple_of((c - n_a) * rows_a, rows_a), rows_a)
                    wb3[tgt, rs, :] = st_a[slot].astype(BF16)
                else:
                    rs = pl.ds(pl.multiple_of((c - 2 * n_a) * rows_b, rows_b), rows_b)
                    wb2[tgt, rs, :] = st_b[slot].astype(BF16)

    def stage(e, c0, cn, tgt):
        def it(k, carry):
            c = c0 + k

            @pl.when(c == 0)
            def _():
                for ahead in range(min(depth, n_chunks)):
                    start_chunk(e, c + ahead)

            finish_chunk(e, c, tgt)

            @pl.when(c + depth < n_chunks)
            def _():
                start_chunk(e, c + depth)

            return carry

        lax.fori_loop(0, cn, it, 0)

    @pl.when(j == 0)
    def _():
        y2[1] = jnp.zeros(y2.shape[1:], y2.dtype)
        dump = pltpu.make_async_copy(y2.at[1], o_hbm.at[pl.ds(o_hbm.shape[0] - 2 * blk, blk)],
                                     ssem.at[0])
        dump.start()
        dump.wait()
        stage(bexp_ref[0], 0, n_chunks, 0)
        _gather_rows(h_hbm, g0_ref, x3.at[0], gsem.at[0])
        _gather_rows(h_hbm, g1_ref, x3.at[1], gsem.at[1])

    @pl.when(j < nu)
    def _():
        xs = j % 3
        ys = j % 2
        ws = wslot_ref[j]
        nxt = (j + 2) % 3
        nq = EXPERT_F_CHUNKS
        fq, rq = f // nq, blk // nq
        _wait_rows(h_hbm, x3.at[xs], gsem.at[xs])

        x = x3[xs].astype(BF16)
        for q in range(nq):
            for r in range(q * rq, (q + 1) * rq):
                pltpu.make_async_copy(y2.at[1 - ys, pl.ds(r, 1)],
                                      o_hbm.at[pl.ds(s_ref[0, 0, r], 1)],
                                      ssem.at[1 - ys]).start(priority=r % 2)
                pltpu.make_async_copy(h_hbm.at[pl.ds(g2_ref[0, 0, r], 1)],
                                      x3.at[nxt, pl.ds(r, 1)], gsem.at[nxt]).start()
            fs = slice(q * fq, (q + 1) * fq)
            a = jnp.dot(x, wb1[ws, :, fs], preferred_element_type=F32)
            b = jnp.dot(x, wb3[ws, :, fs], preferred_element_type=F32)
            hm_scr[:, fs] = ((a * jax.nn.sigmoid(a)) * b).astype(BF16)

        @pl.when(j >= 1)
        def _():
            _wait_rows(o_hbm, y2.at[ys], ssem.at[ys])

        y2[ys] = jnp.dot(hm_scr[...], wb2[ws], preferred_element_type=F32)

        stage(se_ref[j], sc0_ref[j], scn_ref[j], 1 - ws)

        @pl.when(j == nu - 1)
        def _():
            _scatter_rows(y2.at[ys], sl_ref, o_hbm, ssem.at[ys])
            _wait_rows(o_hbm, y2.at[ys], ssem.at[ys])
            _wait_rows(o_hbm, y2.at[1 - ys], ssem.at[1 - ys])
            _wait_rows(h_hbm, x3.at[nxt], gsem.at[nxt])
            _wait_rows(h_hbm, x3.at[(j + 1) % 3], gsem.at[(j + 1) % 3])


def _expert_plan(block_exp, counts_padded, n_used, n_chunks, blk):
    ne = counts_padded.shape[0]
    nb = block_exp.shape[0]
    nblk = counts_padded // blk
    first = (jnp.cumsum(counts_padded) - counts_padded) // blk
    nonempty = nblk > 0
    ordinal = jnp.cumsum(nonempty.astype(jnp.int32)) - nonempty.astype(jnp.int32)
    cand = jnp.where(nonempty, jnp.arange(ne, dtype=jnp.int32), ne)
    suffix_min = jnp.flip(lax.cummin(jnp.flip(cand)))
    nxt = jnp.concatenate([suffix_min[1:], jnp.full((1,), ne, jnp.int32)])
    jj = jnp.arange(nb, dtype=jnp.int32)
    onehot = block_exp[:, None] == jnp.arange(ne, dtype=jnp.int32)[None, :]
    pick = lambda tab: jnp.sum(jnp.where(onehot, tab.astype(jnp.int32)[None, :], 0), axis=1)
    nxt_b = pick(nxt)
    has_b = nxt_b < ne
    m = jj - pick(first)
    n_e = jnp.maximum(pick(nblk), 1)
    quota = (n_chunks + n_e - 1) // n_e
    c0 = jnp.clip(m * quota, 0, n_chunks)
    cn = jnp.clip(n_chunks - c0, 0, quota)
    cn = jnp.where(has_b & (jj < n_used[0]), cn, 0)
    stage_e = jnp.where(has_b, nxt_b, 0)
    wslot = pick(ordinal) % 2
    i32 = lambda v: v.astype(jnp.int32)
    return i32(wslot), i32(stage_e), i32(c0), i32(cn)


def _experts(h2, row_tok, row_dst, block_exp, counts_padded, n_used, w1, w3, w2):
    t, d = h2.shape
    nb = block_exp.shape[0]
    blk = MOE_BLOCK
    f = w1.shape[2]
    rows_a = min(d, max(8, WEIGHT_CHUNK_ELEMS // f))
    rows_b = min(f, max(8, WEIGHT_CHUNK_ELEMS // d))
    n_chunks = 2 * (d // rows_a) + f // rows_b
    wslot, stage_e, c0, cn = _expert_plan(block_exp, counts_padded, n_used, n_chunks, blk)
    gidx = row_tok.reshape(nb, 1, blk)
    dump1 = 2 * t + blk + jnp.arange(blk, dtype=jnp.int32)
    sidx = jnp.concatenate([dump1, row_dst]).reshape(nb + 1, 1, blk)
    last = nb - 1
    smem = lambda m: pl.BlockSpec((1, 1, blk), m, memory_space=pltpu.SMEM)
    anyspec = pl.BlockSpec(memory_space=pl.ANY)
    grid_spec = pltpu.PrefetchScalarGridSpec(
        num_scalar_prefetch=6,
        grid=(nb,),
        in_specs=[smem(lambda j, *_: (0, 0, 0)),
                  smem(lambda j, *_: (min(1, last), 0, 0)),
                  smem(lambda j, *_: (jnp.minimum(j + 2, last), 0, 0)),
                  smem(lambda j, *_: (j, 0, 0)),
                  smem(lambda j, *_: (j + 1, 0, 0)),
                  anyspec, anyspec, anyspec, anyspec],
        out_specs=anyspec,
        scratch_shapes=[pltpu.VMEM((3, blk, d), F32),
                        pltpu.VMEM((2, blk, d), F32),
                        pltpu.VMEM((blk, f), BF16),
                        pltpu.VMEM((2, d, f), BF16), pltpu.VMEM((2, d, f), BF16),
                        pltpu.VMEM((2, f, d), BF16),
                        pltpu.VMEM((WEIGHT_CHUNKS_IN_FLIGHT, rows_a, f), F32),
                        pltpu.VMEM((WEIGHT_CHUNKS_IN_FLIGHT, rows_b, d), F32),
                        pltpu.SemaphoreType.DMA((3,)), pltpu.SemaphoreType.DMA((2,)),
                        pltpu.SemaphoreType.DMA((2 * WEIGHT_CHUNKS_IN_FLIGHT,))],
    )
    return pl.pallas_call(
        _expert_kernel,
        grid_spec=grid_spec,
        out_shape=jax.ShapeDtypeStruct((2 * t + 2 * blk, d), F32),
        compiler_params=_params(("arbitrary",)),
        name="experts",
    )(block_exp, n_used, wslot, stage_e, c0, cn, gidx, gidx, gidx, sidx, sidx, h2, w1, w3, w2)


def _final_kernel(p0_ref, p1_ref, xm_ref, w_ref, g2_ref, lg_ref, lb_ref, o_ref, *, alpha, rc):
    g2 = g2_ref[0]
    lg = lg_ref[...]
    lb = lb_ref[...]

    for r in range(xm_ref.shape[0] // rc):
        rows = pl.ds(r * rc, rc)
        w = w_ref[rows, :]
        y2 = w[:, 0:1] * p0_ref[rows, :] + w[:, 1:2] * p1_ref[rows, :]
        o_ref[rows, :] = _ln_rows(alpha * xm_ref[rows, :] + g2 * y2) * lg + lb


def _final(packed, x_mid, wts, gate2, l2g, l2b, seq, alpha, tm=512):
    t, d = x_mid.shape
    nt = t // tm
    kern = functools.partial(_final_kernel, alpha=alpha, rc=32)
    full = lambda a: pl.BlockSpec(a.shape, lambda i: (0,) * a.ndim)
    return pl.pallas_call(
        kern,
        grid=(nt,),
        in_specs=[pl.BlockSpec((tm, d), lambda i: (i, 0)),
                  pl.BlockSpec((tm, d), lambda i: (i + nt, 0)),
                  pl.BlockSpec((tm, d), lambda i: (i, 0)),
                  pl.BlockSpec((tm, 2), lambda i: (i, 0)),
                  pl.BlockSpec((1, 1, d), lambda i: ((i * tm) // seq, 0, 0)),
                  full(l2g), full(l2b)],
        out_specs=pl.BlockSpec((tm, d), lambda i: (i, 0)),
        out_shape=jax.ShapeDtypeStruct((t, d), F32),
        compiler_params=_params(("arbitrary",)),
        name="final",
    )(packed, packed, x_mid, wts, gate2, l2g, l2b)


def kernel(x, c, ctx, c_ctx, w_ada, b_ada, w_in, w_f_proj, lam_re, lam_im, log_dt, b_re, b_im,
           c_re, c_im, d_skip, w_glu, b_glu, w_s_proj, w_out, ln1_g, ln1_b, w_group, b_group,
           w_expert, b_expert, w1, w3, w2, ln2_g, ln2_b):
    depth = w_ada.shape[0]
    assert depth == 1, "single-layer block"
    bsz, seq, d = x.shape
    ctx_len = ctx.shape[1]
    t = bsz * seq
    fw = w_f_proj.shape[1]
    sw = w_s_proj.shape[1]
    gd = fw // F_GROUPS
    rows = seq // GRID_W
    alpha = (2.0 * depth) ** 0.25
    row2 = lambda v: v.reshape(1, -1).astype(F32)

    cc = jnp.concatenate([c, c_ctx[None], jnp.zeros((8 - bsz - 1, d), F32)], axis=0)
    mod = _adaln(cc, w_ada[0], b_ada[0].reshape(1, -1))
    mods = [mod[:, k * d:(k + 1) * d] for k in range(N_MOD)]
    per_b = lambda m: m[:bsz].reshape(bsz, 1, d)
    ctx_row = lambda m: m[bsz:bsz + 1].reshape(1, 1, d)

    wi = w_in[0].astype(BF16)
    tn = sw
    n_sig = (2 * d) // tn
    main, us = _ln_mod_matmul(x.reshape(t, d), per_b(mods[0]), per_b(mods[1]), wi,
                              rows_per_mod=seq, n_sig=n_sig, tm=1024, tn=tn,
                              nj=wi.shape[1] // tn, col_shift=(fw + sw) // tn)
    _, us_ctx = _ln_mod_matmul(ctx.reshape(bsz * ctx_len, d), ctx_row(mods[0]), ctx_row(mods[1]),
                               wi, rows_per_mod=bsz * ctx_len, n_sig=0, tm=ctx_len, tn=tn,
                               nj=1, col_shift=fw // tn)

    f_out = _fourier_mix(main, 2 * d, bsz, rows, GRID_W, gd)

    perm = _octet_perm()
    tables = _s5_tables(lam_re[0], lam_im[0], log_dt[0], b_re[0], b_im[0], c_re[0], c_im[0],
                        d_skip[0])
    zl = _relayout_in(us, perm)
    zc = _relayout_in(us_ctx, perm)
    y_chunks = _s5_core(zl, zc, tables, bsz)
    y_s = _relayout_out(y_chunks, perm)

    wr = jnp.concatenate([w_group[0], w_expert[0],
                          jnp.zeros((d, ROUTE_LANES - N_EXPERT_GROUPS - N_EXPERTS), F32)], axis=1)
    br = jnp.concatenate([b_group[0], b_expert[0],
                          jnp.zeros((ROUTE_LANES - N_EXPERT_GROUPS - N_EXPERTS,), F32)])
    x_mid, h2, logits = _merge(
        f_out, y_s, main, x.reshape(t, d), per_b(mods[2]), per_b(mods[3]), per_b(mods[4]),
        w_glu[0].astype(BF16), row2(b_glu[0]), w_f_proj[0].astype(BF16), w_s_proj[0].astype(BF16),
        w_out[0].astype(BF16), row2(ln1_g[0]), row2(ln1_b[0]), wr.astype(BF16), row2(br),
        seq, alpha)

    idx_t, wts, cnt = _route(logits)
    experts = idx_t[0:2].astype(jnp.int32)
    rank = idx_t[2:4].astype(jnp.int32)
    counts = cnt[0, :N_EXPERTS].astype(jnp.int32)
    blk = MOE_BLOCK
    n_assign = 2 * t
    nb = -(-(n_assign + N_EXPERTS * (blk - 1)) // blk)
    padded = (counts + blk - 1) // blk * blk
    pend = jnp.cumsum(padded)
    pstart = pend - padded
    eids = jnp.arange(N_EXPERTS, dtype=jnp.int32)[:, None, None]
    dest = jnp.sum(jnp.where(experts[None] == eids, pstart[:, None, None], 0), axis=0) + rank
    n_used = (pend[-1] // blk).astype(jnp.int32).reshape(1)
    block_row = jnp.arange(nb, dtype=jnp.int32) * blk
    block_exp = jnp.minimum(jnp.sum((pend[None, :] <= block_row[:, None]).astype(jnp.int32), axis=1),
                            N_EXPERTS - 1)
    assign = jnp.zeros((nb * blk,), jnp.int32).at[dest.reshape(-1)].add(
        jnp.arange(1, n_assign + 1, dtype=jnp.int32)) - 1
    valid = assign >= 0
    row_tok = jnp.where(valid, assign % t, 0)
    row_in_2blk = jnp.arange(nb * blk, dtype=jnp.int32) % (2 * blk)
    row_dst = jnp.where(valid, assign, 2 * t + row_in_2blk)

    packed = _experts(h2, row_tok, row_dst, block_exp, padded, n_used, w1[0], w3[0], w2[0])
    out = _final(packed, x_mid, wts, per_b(mods[5]), row2(ln2_g[0]), row2(ln2_b[0]), seq, alpha)
    return out.reshape(bsz, seq, d)
```
